```python
import jax, jax.numpy as jnp
from jax import lax
import numpy as np

D_MODEL = 1024
BATCH = 16
SEQ = 2048
DEPTH = 1
DEC_BATCH = 128
DEC_SEQ = 4
PAST_LEN = 8192
PAGE_SIZE = 128

H_RET = 8
DK_RET = 64
DV_RET = 64
RET_CHUNK = 128
RET_W = H_RET * DV_RET
H_MOBA = 8
HD_MOBA = 64
MOBA_BLOCK = 256
MOBA_TOPK = 3
MOBA_QCHUNK = 64
MOBA_W = H_MOBA * HD_MOBA
MIX_W = RET_W + MOBA_W
IN_SPLITS = (H_RET * DK_RET, 2 * H_RET * DK_RET, 2 * H_RET * DK_RET + RET_W, 2 * H_RET * DK_RET + 2 * RET_W,
             2 * H_RET * DK_RET + 2 * RET_W + MOBA_W, 2 * H_RET * DK_RET + 2 * RET_W + 2 * MOBA_W)
IN_W = 2 * H_RET * DK_RET + 2 * RET_W + 3 * MOBA_W
N_MEM = 256
H_MEM = 4
HD_MEM = 128
PEER_HEADS = 8
PEER_NKEYS = 128
PEER_NEXPERTS = PEER_NKEYS * PEER_NKEYS
PEER_DQ = 256
PEER_TOPK = 16
PEER_TOKEN_BLOCK = 128
RMS_EPS = 1e-6
GN_EPS = 1e-5

kernel_name = 'hymba_retnet_moba_peer_step'

F32 = jnp.float32


def rms_norm(x, g):
    xf = x.astype(F32)
    y = xf * lax.rsqrt(jnp.mean(xf * xf, axis=-1, keepdims=True) + RMS_EPS)
    return (y * g.astype(F32)).astype(x.dtype)


def retention_log_decay():
    h = jnp.arange(H_RET, dtype=F32)
    return jnp.log1p(-jnp.exp2(-5.0 - h))


def alibi_slopes(n):
    return jnp.exp2(-8.0 * (jnp.arange(n, dtype=F32) + 1.0) / n)


def retention_chunk(q, k, v, state, log_g):
    T = q.shape[1]
    pos = jnp.arange(T, dtype=F32)
    diff = pos[:, None] - pos[None, :]
    causal = diff >= 0
    decay = jnp.where(causal[None], jnp.exp(log_g[:, None, None] * jnp.where(causal, diff, 0.0)[None]), 0.0)
    scores = jnp.einsum('bthd,bshd->bhts', q, k) * decay[None]
    inner = jnp.einsum('bhts,bshe->bthe', scores, v)
    cross = jnp.einsum('bthd,bhde->bthe', q, state) * jnp.exp(log_g[None, :] * (pos[:, None] + 1.0))[None, :, :, None]
    k_dec = k * jnp.exp(log_g[None, :] * (T - 1.0 - pos[:, None]))[None, :, :, None]
    new_state = state * jnp.exp(log_g * T)[None, :, None, None] + jnp.einsum('bthd,bthe->bhde', k_dec, v)
    return inner + cross, new_state


def retention_prompt(q, k, v, log_g):
    B, S = q.shape[:2]
    n_chunks = S // RET_CHUNK

    def to_chunks(a):
        return a.reshape(B, n_chunks, RET_CHUNK, *a.shape[2:]).swapaxes(0, 1)

    def step(state, inp):
        qc, kc, vc = inp
        o, state = retention_chunk(qc, kc, vc, state, log_g)
        return state, o

    s0 = jnp.zeros((B, H_RET, DK_RET, DV_RET), F32)
    s_fin, o = lax.scan(step, s0, (to_chunks(q), to_chunks(k), to_chunks(v)))
    return o.swapaxes(0, 1).reshape(B, S, H_RET, DV_RET), s_fin


def to_head_blocks(x, n_blocks):
    pad = n_blocks * MOBA_BLOCK - x.shape[0]
    xp = jnp.pad(x, ((0, pad), (0, 0), (0, 0)))
    return xp.reshape(n_blocks, MOBA_BLOCK, H_MOBA, HD_MOBA).transpose(2, 0, 1, 3)


def moba_attend(q, qpos, kt, vt, kmean, slopes):
    Tq = q.shape[0]
    nb = kt.shape[1]
    own = qpos // MOBA_BLOCK
    gate = jnp.einsum('thd,hnd->thn', q.astype(F32), kmean)
    past = jnp.arange(nb)[None, :] < own[:, None]
    gate = jnp.where(past[:, None, :], gate, -jnp.inf)
    _, sel = lax.top_k(gate, min(MOBA_TOPK, nb))
    sel_valid = sel < own[:, None, None]
    blocks = jnp.concatenate([sel, jnp.broadcast_to(own[:, None, None], (Tq, H_MOBA, 1))], axis=-1)
    valid = jnp.concatenate([sel_valid, jnp.ones((Tq, H_MOBA, 1), bool)], axis=-1)
    head = jnp.arange(H_MOBA)[None, :, None]
    kg = kt[head, blocks]
    vg = vt[head, blocks]
    s = jnp.einsum('thd,thnkd->thnk', q, kg).astype(F32)
    kpos = blocks[..., None] * MOBA_BLOCK + jnp.arange(MOBA_BLOCK, dtype=jnp.int32)
    dist = qpos[:, None, None, None] - kpos
    mask = valid[..., None] & (dist >= 0)
    s = jnp.where(mask, s - slopes[None, :, None, None] * dist.astype(F32), -jnp.inf)
    p = jax.nn.softmax(s.reshape(Tq, H_MOBA, -1), axis=-1).reshape(s.shape)
    return jnp.einsum('thnk,thnkd->thd', p.astype(vg.dtype), vg)


def moba_prompt(q, k, v, slopes):
    B, S = q.shape[:2]
    nb = -(-S // MOBA_BLOCK)
    nq = S // MOBA_QCHUNK
    qpos = jnp.arange(S, dtype=jnp.int32).reshape(nq, MOBA_QCHUNK)

    def per_seq(args):
        qs, ks, vs = args
        kt = to_head_blocks(ks, nb)
        vt = to_head_blocks(vs, nb)
        kmean = jnp.mean(kt.astype(F32), axis=2)
        qc = qs.reshape(nq, MOBA_QCHUNK, H_MOBA, HD_MOBA)
        o = lax.map(lambda a: moba_attend(a[0], a[1], kt, vt, kmean, slopes), (qc, qpos))
        return o.reshape(S, H_MOBA, HD_MOBA)

    return lax.map(per_seq, (q, k, v))


def moba_sample(q, k_new, v_new, cache_k, cache_v, page_table, slopes):
    T = q.shape[1]
    nb = -(-(PAST_LEN + T) // MOBA_BLOCK)
    qpos = PAST_LEN + jnp.arange(T, dtype=jnp.int32)

    def per_seq(args):
        qs, kn, vn, pages = args
        kp = cache_k[pages].reshape(PAST_LEN, H_MOBA, HD_MOBA)
        vp = cache_v[pages].reshape(PAST_LEN, H_MOBA, HD_MOBA)
        kt = to_head_blocks(jnp.concatenate([kp, kn.astype(kp.dtype)], axis=0), nb)
        vt = to_head_blocks(jnp.concatenate([vp, vn.astype(vp.dtype)], axis=0), nb)
        kmean = jnp.mean(kt.astype(F32), axis=2)
        return moba_attend(qs, qpos, kt, vt, kmean, slopes)

    return lax.map(per_seq, (q, k_new, v_new, page_table))


def mix_project(x, norm1_g, w_in, moba_qn_g, moba_kn_g):
    B, S, _ = x.shape
    z = rms_norm(x, norm1_g) @ w_in
    rq, rk, rv, rg, mq, mk, mv = jnp.split(z, IN_SPLITS, axis=-1)
    rq = rq.reshape(B, S, H_RET, DK_RET).astype(F32)
    rk = rk.reshape(B, S, H_RET, DK_RET).astype(F32) * (DK_RET ** -0.5)
    rv = rv.reshape(B, S, H_RET, DV_RET).astype(F32)
    mq = rms_norm(mq.reshape(B, S, H_MOBA, HD_MOBA), moba_qn_g) * (HD_MOBA ** -0.5)
    mk = rms_norm(mk.reshape(B, S, H_MOBA, HD_MOBA), moba_kn_g)
    mv = mv.reshape(B, S, H_MOBA, HD_MOBA)
    return rq, rk, rv, rg, mq, mk, mv


def mix_output(x, ret_o, rg, moba_o, ret_gn_g, ret_gn_b, w_out):
    B, S, _ = x.shape
    mu = jnp.mean(ret_o, axis=-1, keepdims=True)
    var = jnp.mean(jnp.square(ret_o - mu), axis=-1, keepdims=True)
    gn = ((ret_o - mu) * lax.rsqrt(var + GN_EPS)).reshape(B, S, RET_W)
    ret_y = (gn * ret_gn_g.astype(F32) + ret_gn_b.astype(F32)) * jax.nn.silu(rg.astype(F32))
    cat = jnp.concatenate([ret_y.astype(x.dtype), moba_o.reshape(B, S, MOBA_W).astype(x.dtype)], axis=-1)
    return x + cat @ w_out


def memory_kv(mem, mem_norm_g, w_mkv, mem_kn_g):
    B, N, _ = mem.shape
    k, v = jnp.split(rms_norm(mem, mem_norm_g) @ w_mkv, 2, axis=-1)
    k = rms_norm(k.reshape(B, N, H_MEM, HD_MEM), mem_kn_g)
    return k, v.reshape(B, N, H_MEM, HD_MEM)


def memory_attend(x, mem_k, mem_v, norm2_g, w_mq, mem_qn_g, w_mo):
    B, S, _ = x.shape
    q = rms_norm((rms_norm(x, norm2_g) @ w_mq).reshape(B, S, H_MEM, HD_MEM), mem_qn_g) * (HD_MEM ** -0.5)
    s = jnp.einsum('bshd,bnhd->bhsn', q, mem_k.astype(q.dtype)).astype(F32)
    p = jax.nn.softmax(s, axis=-1)
    o = jnp.einsum('bhsn,bnhd->bshd', p.astype(mem_v.dtype), mem_v)
    return o.reshape(B, S, H_MEM * HD_MEM).astype(x.dtype) @ w_mo


def peer_ffn(x, norm3_g, wq, sub_keys, u, v):
    B, S, D = x.shape
    T = B * S
    h = rms_norm(x, norm3_g).reshape(T, D)
    nblk = -(-T // PEER_TOKEN_BLOCK)
    hp = jnp.pad(h, ((0, nblk * PEER_TOKEN_BLOCK - T), (0, 0))).reshape(nblk, PEER_TOKEN_BLOCK, D)
    K = PEER_TOPK

    def block(hb):
        n = hb.shape[0]
        q = (hb @ wq).reshape(n, PEER_HEADS, 2, PEER_DQ // 2).astype(F32)
        s = jnp.einsum('thcd,hcnd->thcn', q, sub_keys.astype(F32))
        sv, si = lax.top_k(s, K)
        cand = sv[:, :, 0, :, None] + sv[:, :, 1, None, :]
        cidx = si[:, :, 0, :, None] * PEER_NKEYS + si[:, :, 1, None, :]
        best, pos = lax.top_k(cand.reshape(n, PEER_HEADS, K * K), K)
        eidx = jnp.take_along_axis(cidx.reshape(n, PEER_HEADS, K * K), pos, axis=-1)
        g = jax.nn.softmax(best, axis=-1)
        ue = jnp.take(u, eidx, axis=0)
        ve = jnp.take(v, eidx, axis=0)
        a = jax.nn.gelu(jnp.einsum('td,thkd->thk', hb, ue).astype(F32), approximate=False)
        return jnp.einsum('thk,thkd->td', (g * a).astype(ve.dtype), ve)

    out = lax.map(block, hp).reshape(nblk * PEER_TOKEN_BLOCK, D)[:T]
    return out.reshape(B, S, D).astype(x.dtype)


def layer_prompt(x, mem, mix_w, mem_w, peer_w, log_g, slopes):
    norm1_g, w_in, ret_gn_g, ret_gn_b, moba_qn_g, moba_kn_g, w_out = mix_w
    norm2_g, mem_norm_g, w_mq, w_mkv, mem_qn_g, mem_kn_g, w_mo = mem_w
    rq, rk, rv, rg, mq, mk, mv = mix_project(x, norm1_g, w_in, moba_qn_g, moba_kn_g)
    ret_o, ret_state = retention_prompt(rq, rk, rv, log_g)
    moba_o = moba_prompt(mq, mk, mv, slopes)
    x = mix_output(x, ret_o, rg, moba_o, ret_gn_g, ret_gn_b, w_out)
    mem_k, mem_v = memory_kv(mem, mem_norm_g, w_mkv, mem_kn_g)
    x = x + memory_attend(x, mem_k, mem_v, norm2_g, w_mq, mem_qn_g, w_mo)
    x = x + peer_ffn(x, *peer_w)
    return x, ret_state.astype(x.dtype), mk, mv, mem_k, mem_v


def layer_sample(x, state_ret, cache_k, cache_v, page_table, mem_k, mem_v, mix_w, mem_w, peer_w, log_g, slopes):
    norm1_g, w_in, ret_gn_g, ret_gn_b, moba_qn_g, moba_kn_g, w_out = mix_w
    norm2_g, _, w_mq, _, mem_qn_g, _, w_mo = mem_w
    rq, rk, rv, rg, mq, mk, mv = mix_project(x, norm1_g, w_in, moba_qn_g, moba_kn_g)
    ret_o, new_state = retention_chunk(rq, rk, rv, state_ret.astype(F32), log_g)
    moba_o = moba_sample(mq, mk, mv, cache_k, cache_v, page_table, slopes)
    x = mix_output(x, ret_o, rg, moba_o, ret_gn_g, ret_gn_b, w_out)
    x = x + memory_attend(x, mem_k, mem_v, norm2_g, w_mq, mem_qn_g, w_mo)
    x = x + peer_ffn(x, *peer_w)
    return x, new_state.astype(state_ret.dtype), mk, mv


def setup_inputs(seed: int = 0) -> dict:
    key = jax.random.key(seed)
    ks = jax.random.split(key, 28)
    n_pages = PAST_LEN // PAGE_SIZE
    n_used = DEC_BATCH * n_pages
    n_pool = n_used + max(1, n_used // 4)

    def nrm(k, shape, scale):
        return jax.random.normal(k, shape, F32) * scale

    def gain(k, shape):
        return 1.0 + 0.02 * jax.random.normal(k, shape, F32)

    page_table = jax.random.permutation(ks[0], n_pool)[:n_used].reshape(DEC_BATCH, n_pages).astype(jnp.int32)
    return {
        'x_prompt': nrm(ks[1], (BATCH, SEQ, D_MODEL), 1.0),
        'x_sample': nrm(ks[2], (DEC_BATCH, DEC_SEQ, D_MODEL), 1.0),
        'state_ret': nrm(ks[3], (DEC_BATCH, H_RET, DK_RET, DV_RET), 0.5),
        'cache_moba_k': nrm(ks[4], (n_pool, PAGE_SIZE, H_MOBA, HD_MOBA), 1.0),
        'cache_moba_v': nrm(ks[5], (n_pool, PAGE_SIZE, H_MOBA, HD_MOBA), 1.0),
        'page_table': page_table,
        'cache_mem_k': nrm(ks[6], (DEC_BATCH, N_MEM, H_MEM, HD_MEM), 1.0),
        'cache_mem_v': nrm(ks[7], (DEC_BATCH, N_MEM, H_MEM, HD_MEM), 1.0),
        'mem_prompt': nrm(ks[8], (BATCH, N_MEM, D_MODEL), 1.0),
        'norm1_g': gain(ks[9], (D_MODEL,)),
        'w_in': nrm(ks[10], (D_MODEL, IN_W), D_MODEL ** -0.5),
        'ret_gn_g': gain(ks[11], (RET_W,)),
        'ret_gn_b': nrm(ks[12], (RET_W,), 0.02),
        'moba_qn_g': gain(ks[13], (HD_MOBA,)),
        'moba_kn_g': gain(ks[14], (HD_MOBA,)),
        'w_out': nrm(ks[15], (MIX_W, D_MODEL), MIX_W ** -0.5),
        'norm2_g': gain(ks[16], (D_MODEL,)),
        'mem_norm_g': gain(ks[17], (D_MODEL,)),
        'w_mq': nrm(ks[18], (D_MODEL, H_MEM * HD_MEM), D_MODEL ** -0.5),
        'w_mkv': nrm(ks[19], (D_MODEL, 2 * H_MEM * HD_MEM), D_MODEL ** -0.5),
        'mem_qn_g': gain(ks[20], (HD_MEM,)),
        'mem_kn_g': gain(ks[21], (HD_MEM,)),
        'w_mo': nrm(ks[22], (H_MEM * HD_MEM, D_MODEL), (H_MEM * HD_MEM) ** -0.5),
        'norm3_g': gain(ks[23], (D_MODEL,)),
        'peer_wq': nrm(ks[24], (D_MODEL, PEER_HEADS * PEER_DQ), D_MODEL ** -0.5),
        'peer_sub_keys': nrm(ks[25], (PEER_HEADS, 2, PEER_NKEYS, PEER_DQ // 2), (PEER_DQ // 2) ** -0.5),
        'peer_u': nrm(ks[26], (PEER_NEXPERTS, D_MODEL), D_MODEL ** -0.5),
        'peer_v': nrm(ks[27], (PEER_NEXPERTS, D_MODEL), 0.5),
    }


def reference(x_prompt, x_sample, state_ret, cache_moba_k, cache_moba_v, page_table, cache_mem_k, cache_mem_v,
              mem_prompt, norm1_g, w_in, ret_gn_g, ret_gn_b, moba_qn_g, moba_kn_g, w_out,
              norm2_g, mem_norm_g, w_mq, w_mkv, mem_qn_g, mem_kn_g, w_mo,
              norm3_g, peer_wq, peer_sub_keys, peer_u, peer_v):
    log_g = retention_log_decay()
    slopes = alibi_slopes(H_MOBA)
    mix_w = (norm1_g, w_in, ret_gn_g, ret_gn_b, moba_qn_g, moba_kn_g, w_out)
    mem_w = (norm2_g, mem_norm_g, w_mq, w_mkv, mem_qn_g, mem_kn_g, w_mo)
    peer_w = (norm3_g, peer_wq, peer_sub_keys, peer_u, peer_v)
    y_prompt, y_sample = x_prompt, x_sample
    for _ in range(DEPTH):
        y_prompt, ret_state_p, moba_k_p, moba_v_p, mem_k_p, mem_v_p = layer_prompt(
            y_prompt, mem_prompt, mix_w, mem_w, peer_w, log_g, slopes)
        y_sample, ret_state_s, moba_k_s, moba_v_s = layer_sample(
            y_sample, state_ret, cache_moba_k, cache_moba_v, page_table, cache_mem_k, cache_mem_v,
            mix_w, mem_w, peer_w, log_g, slopes)
    return (y_prompt, y_sample, ret_state_p, moba_k_p, moba_v_p, mem_k_p, mem_v_p, ret_state_s, moba_k_s, moba_v_s)
```

```python
import functools
import math

import jax
import jax.numpy as jnp
from jax import lax
from jax.experimental import pallas as pl
from jax.experimental.pallas import tpu as pltpu
from jax.experimental.pallas import tpu_sc as plsc

F32 = jnp.float32
BF16 = jnp.bfloat16
I32 = jnp.int32

D_MODEL = 1024
H_RET = 8
DK_RET = 64
RET_CHUNK = 128
RET_W = 512
H_MOBA = 8
HD_MOBA = 64
MOBA_BLOCK = 256
MOBA_TOPK = 3
MOBA_W = 512
IN_W = 3584
N_MEM = 256
H_MEM = 4
HD_MEM = 128
MEM_W = 512
PAGE_SIZE = 128
PEER_HEADS = 8
PEER_NKEYS = 128
PEER_DQ = 256
PEER_TOPK = 16
PEER_E = PEER_HEADS * PEER_TOPK
RMS_EPS = 1e-6
GN_EPS = 1e-5

LANES = 128
SC_CORES = 2
SC_SUBCORES = 16
SC_LANES = 16
SC_WORKERS = SC_CORES * SC_SUBCORES
VMEM_LIMIT = 56 * 1024 * 1024

NEG = -1e30
NT_DIMS = (((1,), (1,)), ((), ()))


def _split_hi_lo(x):
    hi = x.astype(BF16)
    lo = (x - hi.astype(F32)).astype(BF16)
    return hi, lo


def _nt(a, b):
    return lax.dot_general(a, b, NT_DIMS, preferred_element_type=F32)


def _nt3(a, b):
    ah, al = _split_hi_lo(a)
    bh, bl = _split_hi_lo(b)
    return _nt(ah, bh) + _nt(ah, bl) + _nt(al, bh)


def _group_sum(x, ones_bd):
    hi, lo = _split_hi_lo(x)
    return (jnp.dot(hi, ones_bd, preferred_element_type=F32)
            + jnp.dot(lo, ones_bd, preferred_element_type=F32))


def _rms_rows(x, g):
    return x * lax.rsqrt(jnp.mean(x * x, axis=-1, keepdims=True) + RMS_EPS) * g


def _block_ones(width, group):
    i = jnp.arange(width) // group
    return (i[:, None] == i[None, :]).astype(BF16)


def _cparams(sem, vmem=None):
    return pltpu.CompilerParams(dimension_semantics=sem, vmem_limit_bytes=vmem)


def _proj_kernel(x_ref, g1_ref, w_ref, qn_ref, kn_ref, ones_ref,
                 rq_ref, rk_ref, rv_ref, rg_ref, mq_ref, mk_ref, mv_ref):
    xn = _rms_rows(x_ref[...], g1_ref[...]).astype(BF16)

    def col(i):
        return jnp.dot(xn, w_ref[:, i * 512:(i + 1) * 512], preferred_element_type=F32)

    rq_ref[...] = col(0)
    rk_ref[...] = col(1) * (DK_RET ** -0.5)
    rv_ref[...] = col(2)
    rg_ref[...] = col(3)
    ones = ones_ref[...]

    def headnorm(a, g):
        ms = _group_sum(a * a, ones) * (1.0 / HD_MOBA)
        return a * lax.rsqrt(ms + RMS_EPS) * g

    mq_ref[...] = headnorm(col(4), qn_ref[...]) * (HD_MOBA ** -0.5)
    mk_ref[...] = headnorm(col(5), kn_ref[...])
    mv_ref[...] = col(6)


def _proj(x2d, norm1_g, w_in_bf, qn512, kn512, ones64, tm):
    T = x2d.shape[0]
    out = jax.ShapeDtypeStruct((T, 512), F32)
    row = lambda i: (i, 0)
    const = lambda i: (0, 0)
    return pl.pallas_call(
        _proj_kernel,
        grid=(T // tm,),
        in_specs=[pl.BlockSpec((tm, D_MODEL), row), pl.BlockSpec((1, D_MODEL), const),
                  pl.BlockSpec((D_MODEL, IN_W), const), pl.BlockSpec((1, 512), const),
                  pl.BlockSpec((1, 512), const), pl.BlockSpec((512, 512), const)],
        out_specs=[pl.BlockSpec((tm, 512), row)] * 7,
        out_shape=[out] * 7,
        compiler_params=_cparams(("parallel",), VMEM_LIMIT),
        name="proj",
    )(x2d, norm1_g.reshape(1, -1), w_in_bf, qn512, kn512, ones64)


def _ret_kernel(q_ref, k_ref, v_ref, s0_ref, dec_ref, rowdec_ref, kdec_ref, sdec_ref, o_ref, st_ref, state_scr):
    c = pl.program_id(1)
    R = q_ref.shape[0]

    @pl.when(c == 0)
    def _():
        state_scr[...] = s0_ref[0]

    lane = lax.broadcasted_iota(I32, (R, LANES), 1)
    lo_lane = lane < 64
    bd = ((lax.broadcasted_iota(I32, (LANES, LANES), 0) < 64)
          == (lax.broadcasted_iota(I32, (LANES, LANES), 1) < 64))
    for p in range(4):
        sl = slice(p * LANES, (p + 1) * LANES)
        q2 = q_ref[:, sl]
        k2 = k_ref[:, sl]
        v2b = v_ref[:, sl].astype(BF16)
        k2b = k2.astype(BF16)
        qa = jnp.where(lo_lane, q2, 0.0).astype(BF16)
        qb = jnp.where(lo_lane, 0.0, q2).astype(BF16)
        sa = _nt(qa, k2b) * dec_ref[2 * p]
        sb = _nt(qb, k2b) * dec_ref[2 * p + 1]
        ia = jnp.dot(sa.astype(BF16), v2b, preferred_element_type=F32)
        ib = jnp.dot(sb.astype(BF16), v2b, preferred_element_type=F32)
        st = state_scr[p]
        cross = jnp.dot(q2.astype(BF16), st.astype(BF16), preferred_element_type=F32) * rowdec_ref[p]
        o_ref[:, sl] = jnp.where(lo_lane, ia, ib) + cross
        kdt = (k2 * kdec_ref[p]).T.astype(BF16)
        upd = jnp.dot(kdt, v2b, preferred_element_type=F32)
        state_scr[p] = jnp.where(bd, st * sdec_ref[p] + upd, 0.0)

    @pl.when(c == pl.num_programs(1) - 1)
    def _():
        st_ref[0] = state_scr[...]


def _ret_tables(t_real, rows):
    h = jnp.arange(H_RET, dtype=F32)
    log_g = jnp.log1p(-jnp.exp2(-5.0 - h))
    pos = jnp.arange(rows, dtype=F32)
    diff = pos[:, None] - pos[None, :]
    ok = (diff >= 0) & (pos[:, None] < t_real) & (pos[None, :] < t_real)
    dec = jnp.where(ok[None], jnp.exp(log_g[:, None, None] * jnp.where(ok, diff, 0.0)[None]), 0.0)
    lg_lane = jnp.repeat(log_g, 64).reshape(4, 1, LANES)
    rowdec = jnp.exp(lg_lane * (pos[None, :, None] + 1.0))
    kdec = jnp.exp(lg_lane * jnp.maximum(t_real - 1.0 - pos, 0.0)[None, :, None])
    sdec = jnp.exp(lg_lane * float(t_real))
    return dec, rowdec, kdec, sdec


def _to_block_diag(state):
    n = state.shape[0]
    s = state.reshape(n, 4, 2, 64, 64)
    z = jnp.zeros_like(s[:, :, 0])
    top = jnp.concatenate([s[:, :, 0], z], axis=-1)
    bot = jnp.concatenate([z, s[:, :, 1]], axis=-1)
    return jnp.concatenate([top, bot], axis=-2)


def _from_block_diag(sbd):
    n = sbd.shape[0]
    a = sbd[:, :, :64, :64]
    b = sbd[:, :, 64:, 64:]
    return jnp.stack([a, b], axis=2).reshape(n, H_RET, 64, 64)


def _retention(q, k, v, state_bd, t_real, rows, n_seq, n_chunks):
    dec, rowdec, kdec, sdec = _ret_tables(t_real, rows)
    T = q.shape[0]
    tok = lambda s, c: (s * n_chunks + c, 0)
    c3 = lambda s, c: (0, 0, 0)
    st = lambda s, c: (s, 0, 0, 0)
    return pl.pallas_call(
        _ret_kernel,
        grid=(n_seq, n_chunks),
        in_specs=[pl.BlockSpec((rows, 512), tok)] * 3 + [
            pl.BlockSpec((1, 4, LANES, LANES), st),
            pl.BlockSpec((H_RET, rows, rows), c3), pl.BlockSpec((4, rows, LANES), c3),
            pl.BlockSpec((4, rows, LANES), c3), pl.BlockSpec((4, 1, LANES), c3)],
        out_specs=[pl.BlockSpec((rows, 512), tok), pl.BlockSpec((1, 4, LANES, LANES), st)],
        out_shape=[jax.ShapeDtypeStruct((T, 512), F32), jax.ShapeDtypeStruct((n_seq, 4, LANES, LANES), F32)],
        scratch_shapes=[pltpu.VMEM((4, LANES, LANES), F32)],
        compiler_params=_cparams(("parallel", "arbitrary")),
        name="retention",
    )(q, k, v, state_bd, dec, rowdec, kdec, sdec)


def _moba_prompt_kernel(q_ref, k_ref, v_ref, alibi_ref, slope_ref, o_ref,
                        kb_scr, vt_scr, km_scr, bias_scr, m_scr, l_scr, acc_scr, o0_scr):
    i = pl.program_id(2)
    nb = km_scr.shape[0]
    B = MOBA_BLOCK

    @pl.when(i == 0)
    def _():
        for j in range(nb):
            kj = k_ref[j * B:(j + 1) * B, :]
            kb_scr[j * B:(j + 1) * B, :] = kj.astype(BF16)
            km_scr[j:j + 1, :] = jnp.sum(kj, axis=0, keepdims=True) * (1.0 / B)
            vt_scr[:, j * B:(j + 1) * B] = v_ref[j * B:(j + 1) * B, :].T.astype(BF16)

    q2 = q_ref[...]
    lane = lax.broadcasted_iota(I32, (B, LANES), 1)
    blk = lax.broadcasted_iota(I32, (nb, B), 0)
    key_r = lax.broadcasted_iota(I32, (B, B), 0)
    qry_c = lax.broadcasted_iota(I32, (B, B), 1)
    km = km_scr[...]

    for head in range(2):
        qh = jnp.where((lane < 64) == (head == 0), q2, 0.0)
        qhb = qh.astype(BF16)
        g = _nt3(km, qh)
        past = blk < i
        g = jnp.where(past, g, -jnp.inf)
        rank = jnp.zeros((nb, B), I32)
        for jj in range(nb):
            gj = g[jj:jj + 1, :]
            beats = (gj > g) | ((gj == g) & (jj < blk))
            rank = rank + beats.astype(I32)
        bias_scr[...] = jnp.where(past & (rank < MOBA_TOPK), 0.0, NEG)

        alibi = alibi_ref[0, head]
        slope_b = slope_ref[0, head]

        s = _nt(kb_scr[pl.ds(pl.multiple_of(i * B, B), B), :], qhb) - alibi
        s = jnp.where(key_r <= qry_c, s, NEG)
        m0 = jnp.max(s, axis=0, keepdims=True)
        p0 = jnp.exp(s - m0)
        m_scr[...] = m0
        l_scr[...] = jnp.sum(p0, axis=0, keepdims=True)
        acc_scr[...] = jnp.dot(vt_scr[:, pl.ds(pl.multiple_of(i * B, B), B)], p0.astype(BF16),
                               preferred_element_type=F32)

        def body(j, carry):
            off = pl.multiple_of(j * B, B)
            rowb = bias_scr[pl.ds(j, 1), :] - slope_b * (i - j).astype(F32)
            sj = _nt(kb_scr[pl.ds(off, B), :], qhb) - alibi + rowb
            m_old = m_scr[...]
            m_new = jnp.maximum(m_old, jnp.max(sj, axis=0, keepdims=True))
            alpha = jnp.exp(m_old - m_new)
            pj = jnp.exp(sj - m_new)
            l_scr[...] = alpha * l_scr[...] + jnp.sum(pj, axis=0, keepdims=True)
            acc_scr[...] = alpha * acc_scr[...] + jnp.dot(vt_scr[:, pl.ds(off, B)], pj.astype(BF16),
                                                          preferred_element_type=F32)
            m_scr[...] = m_new
            return carry

        lax.fori_loop(0, i, body, 0)
        res = acc_scr[...] * (1.0 / l_scr[...])
        if head == 0:
            o0_scr[...] = res
        else:
            row = lax.broadcasted_iota(I32, (LANES, B), 0)
            o_ref[...] = jnp.where(row < 64, o0_scr[...], res).T


def _moba_tables():
    slopes = jnp.exp2(-8.0 * (jnp.arange(H_MOBA, dtype=F32) + 1.0) / H_MOBA)
    r = jnp.arange(MOBA_BLOCK, dtype=F32)
    rel = r[None, :] - r[:, None]
    alibi = (slopes[:, None, None] * rel[None]).reshape(4, 2, MOBA_BLOCK, MOBA_BLOCK)
    slope_b = jnp.broadcast_to((slopes * MOBA_BLOCK)[:, None, None], (H_MOBA, 1, MOBA_BLOCK))
    return slopes, alibi, slope_b.reshape(4, 2, 1, MOBA_BLOCK)


def _moba_prompt(mq, mk, mv, n_seq, seq):
    _, alibi, slope_b = _moba_tables()
    nq = seq // MOBA_BLOCK
    T = mq.shape[0]
    B = MOBA_BLOCK
    return pl.pallas_call(
        _moba_prompt_kernel,
        grid=(n_seq, 4, nq),
        in_specs=[pl.BlockSpec((B, LANES), lambda b, p, i: (b * nq + i, p)),
                  pl.BlockSpec((seq, LANES), lambda b, p, i: (b, p)),
                  pl.BlockSpec((seq, LANES), lambda b, p, i: (b, p)),
                  pl.BlockSpec((1, 2, B, B), lambda b, p, i: (p, 0, 0, 0)),
                  pl.BlockSpec((1, 2, 1, B), lambda b, p, i: (p, 0, 0, 0))],
        out_specs=pl.BlockSpec((B, LANES), lambda b, p, i: (b * nq + i, p)),
        out_shape=jax.ShapeDtypeStruct((T, 512), F32),
        scratch_shapes=[pltpu.VMEM((seq, LANES), BF16), pltpu.VMEM((LANES, seq), BF16),
                        pltpu.VMEM((nq, LANES), F32), pltpu.VMEM((nq, B), F32),
                        pltpu.VMEM((1, B), F32), pltpu.VMEM((1, B), F32),
                        pltpu.VMEM((LANES, B), F32), pltpu.VMEM((LANES, B), F32)],
        compiler_params=_cparams(("parallel", "parallel", "arbitrary")),
        name="moba_prompt",
    )(mq, mk, mv, alibi, slope_b)


def _moba_sample_kernel(pt_ref, q_ref, kn_ref, vn_ref, kp_ref, vp_ref, alibi_ref, slopec_ref, hm_ref, trow_ref,
                        o_ref, qbd_scr, ksum_scr, knew_scr, vnew_scr, m_scr, l_scr, o_scr, *, past_len):
    p = pl.program_id(1)
    n_pages = pl.num_programs(1)
    R = qbd_scr.shape[0]
    lanei = lax.broadcasted_iota(I32, (R, LANES), 1)

    @pl.when(p == 0)
    def _():
        q = q_ref[0]
        hm = hm_ref[...]
        for t in range(4):
            qbd_scr[t * 8:(t + 1) * 8, :] = jnp.broadcast_to(q[t:t + 1, :], (8, 512)) * hm
        ksum_scr[...] = jnp.zeros_like(ksum_scr)
        knew_scr[...] = jnp.zeros_like(knew_scr)
        vnew_scr[...] = jnp.zeros_like(vnew_scr)
        knew_scr[0:4, :] = kn_ref[0]
        vnew_scr[0:4, :] = vn_ref[0]
        m_scr[...] = jnp.zeros_like(m_scr)
        l_scr[...] = jnp.zeros_like(l_scr)

    qbd = qbd_scr[...]
    qbd_b = qbd.astype(BF16)
    kp = kp_ref[0]
    rowk = lax.broadcasted_iota(I32, ksum_scr.shape, 0)
    ksum = ksum_scr[...]
    ksum_scr[...] = jnp.where((rowk >> 1) == (p >> 1), ksum + jnp.sum(kp, axis=0, keepdims=True), ksum)
    off = (past_len - p * PAGE_SIZE).astype(F32)
    s = _nt(qbd_b, kp.astype(BF16)) - alibi_ref[...] - slopec_ref[...] * off
    m = jnp.max(s, axis=1, keepdims=True)
    e = jnp.exp(s - m)
    m_scr[...] = jnp.where(lanei == p, m, m_scr[...])
    l_scr[...] = jnp.where(lanei == p, jnp.sum(e, axis=1, keepdims=True), l_scr[...])
    o_scr[p] = jnp.dot(e.astype(BF16), vp_ref[0].astype(BF16), preferred_element_type=F32)

    @pl.when(p == n_pages - 1)
    def _():
        km = ksum_scr[...] * (1.0 / MOBA_BLOCK)
        g = _nt3(qbd, km)
        n_pg = past_len // PAGE_SIZE
        g = jnp.where((lanei < n_pg) & ((lanei & 1) == 0), g, -jnp.inf)
        sel = lanei < 0
        for _ in range(MOBA_TOPK):
            mx = jnp.max(g, axis=1, keepdims=True)
            ix = jnp.min(jnp.where(g == mx, lanei, LANES), axis=1, keepdims=True)
            pick = (lanei == ix) | (lanei == ix + 1)
            sel = sel | pick
            g = jnp.where(pick, -jnp.inf, g)
        sn = _nt(qbd_b, knew_scr[...].astype(BF16)) - alibi_ref[...]
        sn = jnp.where(lanei <= trow_ref[...], sn, NEG)
        mn = jnp.max(sn, axis=1, keepdims=True)
        en = jnp.exp(sn - mn)
        ln = jnp.sum(en, axis=1, keepdims=True)
        on = jnp.dot(en.astype(BF16), vnew_scr[...].astype(BF16), preferred_element_type=F32)
        m_all = m_scr[...]
        mt = jnp.maximum(jnp.max(jnp.where(sel, m_all, -jnp.inf), axis=1, keepdims=True), mn)
        w_all = jnp.where(sel, jnp.exp(m_all - mt), 0.0)
        wn = jnp.exp(mn - mt)
        den = jnp.sum(w_all * l_scr[...], axis=1, keepdims=True) + wn * ln
        num = wn * on
        for pg in range(n_pg):
            num = num + w_all[:, pg:pg + 1] * o_scr[pg]
        of = num * (1.0 / den)
        o_ref[0] = jnp.sum(of.reshape(4, 8, 512) * hm_ref[...][None], axis=1)


def _moba_sample(mq, mk, mv, cache_k, cache_v, page_table, past_len):
    n_seq, n_pages = page_table.shape
    slopes = jnp.exp2(-8.0 * (jnp.arange(H_MOBA, dtype=F32) + 1.0) / H_MOBA)
    row = jnp.arange(32)
    t_row = (row // 8).astype(F32)
    s_row = slopes[row % 8]
    lane = jnp.arange(LANES, dtype=F32)
    alibi = s_row[:, None] * (t_row[:, None] - lane[None, :])
    slopec = jnp.broadcast_to(s_row[:, None], (32, LANES))
    hm = (jnp.arange(512)[None, :] // 64 == jnp.arange(8)[:, None]).astype(F32)
    trow = jnp.broadcast_to((row // 8).astype(I32)[:, None], (32, LANES))
    n_pool = cache_k.shape[0]
    ck = cache_k.reshape(n_pool, PAGE_SIZE, 512)
    cv = cache_v.reshape(n_pool, PAGE_SIZE, 512)
    seq3 = lambda b, p, pt: (b, 0, 0)
    page = lambda b, p, pt: (pt[b * n_pages + p], 0, 0)
    c2 = lambda b, p, pt: (0, 0)
    grid_spec = pltpu.PrefetchScalarGridSpec(
        num_scalar_prefetch=1,
        grid=(n_seq, n_pages),
        in_specs=[pl.BlockSpec((1, 4, 512), seq3), pl.BlockSpec((1, 4, 512), seq3), pl.BlockSpec((1, 4, 512), seq3),
                  pl.BlockSpec((1, PAGE_SIZE, 512), page), pl.BlockSpec((1, PAGE_SIZE, 512), page),
                  pl.BlockSpec((32, LANES), c2), pl.BlockSpec((32, LANES), c2), pl.BlockSpec((8, 512), c2),
                  pl.BlockSpec((32, LANES), c2)],
        out_specs=pl.BlockSpec((1, 4, 512), seq3),
        scratch_shapes=[pltpu.VMEM((32, 512), F32), pltpu.VMEM((LANES, 512), F32),
                        pltpu.VMEM((LANES, 512), F32), pltpu.VMEM((LANES, 512), F32),
                        pltpu.VMEM((32, LANES), F32), pltpu.VMEM((32, LANES), F32),
                        pltpu.VMEM((n_pages, 32, 512), F32)],
    )
    out = pl.pallas_call(
        functools.partial(_moba_sample_kernel, past_len=past_len),
        grid_spec=grid_spec,
        out_shape=jax.ShapeDtypeStruct((n_seq, 4, 512), F32),
        compiler_params=_cparams(("parallel", "arbitrary")),
        name="moba_sample",
    )(page_table.reshape(-1), mq.reshape(n_seq, 4, 512), mk.reshape(n_seq, 4, 512), mv.reshape(n_seq, 4, 512),
      ck, cv, alibi, slopec, hm, trow)
    return out.reshape(n_seq * 4, 512)


def _mix_kernel(x_ref, ro_ref, rg_ref, mo_ref, gng_ref, gnb_ref, wout_ref, n2_ref, wmq_ref, qn_ref, ones_ref,
                x1_ref, qm_ref):
    ones = ones_ref[...]
    ro = ro_ref[...]
    mu = _group_sum(ro, ones) * (1.0 / 64)
    d = ro - mu
    var = _group_sum(d * d, ones) * (1.0 / 64)
    gn = d * lax.rsqrt(var + GN_EPS)
    rg = rg_ref[...]
    ret_y = (gn * gng_ref[...] + gnb_ref[...]) * (rg * jax.nn.sigmoid(rg))
    y = (jnp.dot(ret_y.astype(BF16), wout_ref[0:512, :], preferred_element_type=F32)
         + jnp.dot(mo_ref[...].astype(BF16), wout_ref[512:1024, :], preferred_element_type=F32))
    x1 = x_ref[...] + y
    x1_ref[...] = x1
    h2 = _rms_rows(x1, n2_ref[...]).astype(BF16)
    qm = jnp.dot(h2, wmq_ref[...], preferred_element_type=F32)
    qn = qn_ref[...]
    for h in range(H_MEM):
        sl = slice(h * HD_MEM, (h + 1) * HD_MEM)
        qm_ref[:, sl] = _rms_rows(qm[:, sl], qn[:, sl]) * (HD_MEM ** -0.5)


def _mix(x2d, ret_o, rg, moba_o, gn_g, gn_b, w_out_bf, norm2_g, w_mq_bf, qn512, ones64, tm):
    T = x2d.shape[0]
    row = lambda i: (i, 0)
    const = lambda i: (0, 0)
    return pl.pallas_call(
        _mix_kernel,
        grid=(T // tm,),
        in_specs=[pl.BlockSpec((tm, D_MODEL), row), pl.BlockSpec((tm, 512), row), pl.BlockSpec((tm, 512), row),
                  pl.BlockSpec((tm, 512), row), pl.BlockSpec((1, 512), const), pl.BlockSpec((1, 512), const),
                  pl.BlockSpec((D_MODEL, D_MODEL), const), pl.BlockSpec((1, D_MODEL), const),
                  pl.BlockSpec((D_MODEL, MEM_W), const), pl.BlockSpec((1, MEM_W), const),
                  pl.BlockSpec((512, 512), const)],
        out_specs=[pl.BlockSpec((tm, D_MODEL), row), pl.BlockSpec((tm, MEM_W), row)],
        out_shape=[jax.ShapeDtypeStruct((T, D_MODEL), F32), jax.ShapeDtypeStruct((T, MEM_W), F32)],
        compiler_params=_cparams(("parallel",), VMEM_LIMIT),
        name="mix_out",
    )(x2d, ret_o, rg, moba_o, gn_g.reshape(1, -1), gn_b.reshape(1, -1), w_out_bf, norm2_g.reshape(1, -1),
      w_mq_bf, qn512, ones64)


def _memkv_kernel(m_ref, g_ref, w_ref, kn_ref, k_ref, v_ref):
    mn = _rms_rows(m_ref[...], g_ref[...]).astype(BF16)
    k = jnp.dot(mn, w_ref[:, 0:MEM_W], preferred_element_type=F32)
    kn = kn_ref[...]
    for h in range(H_MEM):
        sl = slice(h * HD_MEM, (h + 1) * HD_MEM)
        k_ref[:, sl] = _rms_rows(k[:, sl], kn[:, sl])
    v_ref[...] = jnp.dot(mn, w_ref[:, MEM_W:2 * MEM_W], preferred_element_type=F32)


def _memkv(mem2d, mem_norm_g, w_mkv_bf, kn512, tm):
    T = mem2d.shape[0]
    row = lambda i: (i, 0)
    const = lambda i: (0, 0)
    out = jax.ShapeDtypeStruct((T, MEM_W), F32)
    return pl.pallas_call(
        _memkv_kernel,
        grid=(T // tm,),
        in_specs=[pl.BlockSpec((tm, D_MODEL), row), pl.BlockSpec((1, D_MODEL), const),
                  pl.BlockSpec((D_MODEL, 2 * MEM_W), const), pl.BlockSpec((1, MEM_W), const)],
        out_specs=[pl.BlockSpec((tm, MEM_W), row)] * 2,
        out_shape=[out, out],
        compiler_params=_cparams(("parallel",)),
        name="mem_kv",
    )(mem2d, mem_norm_g.reshape(1, -1), w_mkv_bf, kn512)


def _memattn_body(q, k, v):
    outs = []
    for h in range(H_MEM):
        sl = slice(h * HD_MEM, (h + 1) * HD_MEM)
        s = _nt(q[:, sl].astype(BF16), k[:, sl].astype(BF16))
        m = jnp.max(s, axis=1, keepdims=True)
        e = jnp.exp(s - m)
        pr = e * (1.0 / jnp.sum(e, axis=1, keepdims=True))
        outs.append(jnp.dot(pr.astype(BF16), v[:, sl].astype(BF16), preferred_element_type=F32))
    return outs


def _memattn_prompt_kernel(q_ref, k_ref, v_ref, o_ref):
    outs = _memattn_body(q_ref[...], k_ref[...], v_ref[...])
    for h in range(H_MEM):
        o_ref[:, h * HD_MEM:(h + 1) * HD_MEM] = outs[h]


def _memattn_prompt(qm, mem_k, mem_v, n_seq, seq, tq):
    T = qm.shape[0]
    nq = seq // tq
    return pl.pallas_call(
        _memattn_prompt_kernel,
        grid=(n_seq, nq),
        in_specs=[pl.BlockSpec((tq, MEM_W), lambda b, i: (b * nq + i, 0)),
                  pl.BlockSpec((N_MEM, MEM_W), lambda b, i: (b, 0)),
                  pl.BlockSpec((N_MEM, MEM_W), lambda b, i: (b, 0))],
        out_specs=pl.BlockSpec((tq, MEM_W), lambda b, i: (b * nq + i, 0)),
        out_shape=jax.ShapeDtypeStruct((T, MEM_W), F32),
        compiler_params=_cparams(("parallel", "arbitrary")),
        name="memattn_prompt",
    )(qm, mem_k, mem_v)


def _memattn_sample_kernel(q_ref, k_ref, v_ref, o_ref):
    outs = _memattn_body(q_ref[0], k_ref[0], v_ref[0])
    for h in range(H_MEM):
        o_ref[0, :, h * HD_MEM:(h + 1) * HD_MEM] = outs[h]


def _memattn_sample(qm, cache_k, cache_v, n_seq, t_dec):
    pad_rows = 16
    q3 = jnp.pad(qm.reshape(n_seq, t_dec, MEM_W), ((0, 0), (0, pad_rows - t_dec), (0, 0)))
    ck = cache_k.reshape(n_seq, N_MEM, MEM_W)
    cv = cache_v.reshape(n_seq, N_MEM, MEM_W)
    b3 = lambda b: (b, 0, 0)
    out = pl.pallas_call(
        _memattn_sample_kernel,
        grid=(n_seq,),
        in_specs=[pl.BlockSpec((1, pad_rows, MEM_W), b3), pl.BlockSpec((1, N_MEM, MEM_W), b3),
                  pl.BlockSpec((1, N_MEM, MEM_W), b3)],
        out_specs=pl.BlockSpec((1, pad_rows, MEM_W), b3),
        out_shape=jax.ShapeDtypeStruct((n_seq, pad_rows, MEM_W), F32),
        compiler_params=_cparams(("parallel",)),
        name="memattn_sample",
    )(q3, ck, cv)
    return out[:, :t_dec].reshape(n_seq * t_dec, MEM_W)


def _topk_rows(s, k, payload=None):
    n = s.shape[0]
    rowf = lax.broadcasted_iota(I32, s.shape, 0).astype(F32)
    vals, idxs, pays = [], [], []
    for _ in range(k):
        mx = jnp.max(s, axis=0, keepdims=True)
        ix = jnp.min(jnp.where(s == mx, rowf, float(n)), axis=0, keepdims=True)
        hit = rowf == ix
        vals.append(mx)
        idxs.append(ix)
        if payload is not None:
            pays.append(jnp.sum(jnp.where(hit, payload, 0.0), axis=0, keepdims=True))
        s = jnp.where(hit, -jnp.inf, s)
    cat = lambda xs: jnp.concatenate(xs, axis=0)
    return cat(vals), cat(idxs), (cat(pays) if payload is not None else None)


def _route_kernel(x1_ref, om_ref, wmo_ref, n3_ref, wq_ref, skh_ref, skl_ref,
                  x2_ref, h3_ref, e_ref, g_ref, q_scr, et_scr, gt_scr):
    x2 = x1_ref[...] + jnp.dot(om_ref[...].astype(BF16), wmo_ref[...], preferred_element_type=F32)
    x2_ref[...] = x2
    h3 = _rms_rows(x2, n3_ref[...])
    h3_ref[...] = h3
    h3b = h3.astype(BF16)
    for j in range(2 * PEER_HEADS):
        q_scr[j] = jnp.dot(h3b, wq_ref[:, j * LANES:(j + 1) * LANES], preferred_element_type=F32)

    K = PEER_TOPK

    def head_body(h, carry):
        sv, si = [], []
        for c in range(2):
            q = q_scr[2 * h + c]
            qh, ql = _split_hi_lo(q)
            kh = skh_ref[2 * h + c]
            s = _nt(kh, qh) + _nt(kh, ql) + _nt(skl_ref[2 * h + c], qh)
            v, ix, _ = _topk_rows(s, K)
            sv.append(v)
            si.append(ix)
        cand = [sv[0][0:1] + sv[1]]
        cidx = [si[0][0:1] * PEER_NKEYS + si[1]]
        for a in range(1, 8):
            cand.append(sv[0][a:a + 1] + sv[1][0:8])
            cidx.append(si[0][a:a + 1] * PEER_NKEYS + si[1][0:8])
        cand.append(sv[0][8:16] + sv[1][0:1])
        cidx.append(si[0][8:16] * PEER_NKEYS + si[1][0:1])
        best, _, eidx = _topk_rows(jnp.concatenate(cand, axis=0), K, jnp.concatenate(cidx, axis=0))
        e = jnp.exp(best - best[0:1])
        gate = e * (1.0 / jnp.sum(e, axis=0, keepdims=True))
        r0 = pl.multiple_of(h * K, K)
        et_scr[pl.ds(r0, K), :] = eidx
        gt_scr[pl.ds(r0, K), :] = gate
        return carry

    lax.fori_loop(0, PEER_HEADS, head_body, 0)
    e_ref[...] = et_scr[...].T.astype(I32)
    g_ref[...] = gt_scr[...].T


def _route(x1, omem, w_mo_bf, norm3_g, wq_bf, sk_hi, sk_lo, tm):
    T = x1.shape[0]
    row = lambda i: (i, 0)
    const = lambda i: (0, 0)
    c3 = lambda i: (0, 0, 0)
    return pl.pallas_call(
        _route_kernel,
        grid=(T // tm,),
        in_specs=[pl.BlockSpec((tm, D_MODEL), row), pl.BlockSpec((tm, MEM_W), row),
                  pl.BlockSpec((MEM_W, D_MODEL), const), pl.BlockSpec((1, D_MODEL), const),
                  pl.BlockSpec((D_MODEL, PEER_HEADS * PEER_DQ), const),
                  pl.BlockSpec((2 * PEER_HEADS, PEER_NKEYS, LANES), c3),
                  pl.BlockSpec((2 * PEER_HEADS, PEER_NKEYS, LANES), c3)],
        out_specs=[pl.BlockSpec((tm, D_MODEL), row), pl.BlockSpec((tm, D_MODEL), row),
                   pl.BlockSpec((tm, PEER_E), row), pl.BlockSpec((tm, PEER_E), row)],
        out_shape=[jax.ShapeDtypeStruct((T, D_MODEL), F32), jax.ShapeDtypeStruct((T, D_MODEL), F32),
                   jax.ShapeDtypeStruct((T, PEER_E), I32), jax.ShapeDtypeStruct((T, PEER_E), F32)],
        scratch_shapes=[pltpu.VMEM((2 * PEER_HEADS, tm, LANES), F32), pltpu.VMEM((PEER_E, tm), F32),
                        pltpu.VMEM((PEER_E, tm), F32)],
        compiler_params=_cparams(("parallel",), VMEM_LIMIT),
        name="peer_route",
    )(x1, omem, w_mo_bf, norm3_g.reshape(1, -1), wq_bf, sk_hi, sk_lo)


PEER_CH = 32


def _sc_mesh():
    return plsc.VectorSubcoreMesh(core_axis_name="c", subcore_axis_name="s")


def _peer_dot_sc(h3, u, eidx, tb):
    T = h3.shape[0]
    tpw = T // SC_WORKERS
    nch = PEER_E // PEER_CH
    idx2 = eidx.reshape(T * nch, PEER_CH)

    @functools.partial(
        pl.kernel, mesh=_sc_mesh(),
        out_type=jax.ShapeDtypeStruct((T * PEER_E,), F32),
        scratch_types=[
            pltpu.VMEM((tb * nch, PEER_CH), I32),
            pltpu.VMEM((tb, D_MODEL), F32),
            pltpu.VMEM((PEER_CH, D_MODEL), F32),
            pltpu.VMEM((PEER_CH, D_MODEL), F32),
            pltpu.VMEM((tb * PEER_E,), F32),
            pltpu.VMEM((SC_LANES * SC_LANES,), F32),
            pltpu.SemaphoreType.DMA,
            pltpu.SemaphoreType.DMA,
        ],
        compiler_params=pltpu.CompilerParams(needs_layout_passes=False),
        name="peer_dot_sc",
    )
    def k(h_hbm, u_hbm, idx_hbm, out_hbm, idx_v, h_v, rows0, rows1, a_v, accm, sem0, sem1):
        wid = lax.axis_index("s") * SC_CORES + lax.axis_index("c")
        tok0 = wid * tpw
        lanes = lax.iota(I32, SC_LANES)

        def gather(g, rows, sem):
            return pltpu.make_async_copy(u_hbm.at[idx_v.at[g]], rows, sem)

        def compute(g, rows):
            tl = g // nch
            for half in range(PEER_CH // SC_LANES):
                def body(c, accs):
                    hc = h_v[tl, pl.ds(c * SC_LANES, SC_LANES)]
                    return tuple(
                        accs[r] + rows[half * SC_LANES + r, pl.ds(c * SC_LANES, SC_LANES)] * hc
                        for r in range(SC_LANES))
                accs = lax.fori_loop(0, D_MODEL // SC_LANES, body,
                                     tuple(jnp.zeros((SC_LANES,), F32) for _ in range(SC_LANES)))
                for r in range(SC_LANES):
                    accm[pl.ds(r * SC_LANES, SC_LANES)] = accs[r]
                tot = jnp.zeros((SC_LANES,), F32)
                for l in range(SC_LANES):
                    tot = tot + plsc.load_gather(accm, [lanes * SC_LANES + l])
                a_v[pl.ds(g * PEER_CH + half * SC_LANES, SC_LANES)] = tot

        @pl.loop(0, tpw // tb)
        def _(blk):
            t0 = tok0 + blk * tb
            pltpu.sync_copy(idx_hbm.at[pl.ds(t0 * nch, tb * nch)], idx_v)
            pltpu.sync_copy(h_hbm.at[pl.ds(t0, tb)], h_v)
            gather(0, rows0, sem0).start()

            @pl.loop(0, tb * nch, step=2)
            def _(g):
                gather(g + 1, rows1, sem1).start()
                gather(g, rows0, sem0).wait()
                compute(g, rows0)

                @pl.when(g + 2 < tb * nch)
                def _():
                    gather(g + 2, rows0, sem0).start()

                gather(g + 1, rows1, sem1).wait()
                compute(g + 1, rows1)

            pltpu.sync_copy(a_v, out_hbm.at[pl.ds(t0 * PEER_E, tb * PEER_E)])

    return k(h3, u, idx2).reshape(T, PEER_E)


def _peer_sum_sc(x2, w, v, eidx, tb):
    T = x2.shape[0]
    tpw = T // SC_WORKERS
    nch = PEER_E // PEER_CH
    idx2 = eidx.reshape(T * nch, PEER_CH)
    CG = 16
    ncg = D_MODEL // (CG * SC_LANES)

    @functools.partial(
        pl.kernel, mesh=_sc_mesh(),
        out_type=jax.ShapeDtypeStruct((T, D_MODEL), F32),
        scratch_types=[
            pltpu.VMEM((tb * nch, PEER_CH), I32),
            pltpu.VMEM((tb * PEER_E,), F32),
            pltpu.VMEM((PEER_CH, D_MODEL), F32),
            pltpu.VMEM((PEER_CH, D_MODEL), F32),
            pltpu.VMEM((tb, D_MODEL), F32),
            pltpu.SemaphoreType.DMA,
            pltpu.SemaphoreType.DMA,
        ],
        compiler_params=pltpu.CompilerParams(needs_layout_passes=False),
        name="peer_sum_sc",
    )
    def k(x_hbm, w_hbm, v_hbm, idx_hbm, out_hbm, idx_v, w_v, rows0, rows1, y_v, sem0, sem1):
        wid = lax.axis_index("s") * SC_CORES + lax.axis_index("c")
        tok0 = wid * tpw
        zero_i = jnp.zeros((SC_LANES,), I32)

        def gather(g, rows, sem):
            return pltpu.make_async_copy(v_hbm.at[idx_v.at[g]], rows, sem)

        def compute(g, rows):
            tl = g // nch
            for cg in range(ncg):
                base = cg * CG * SC_LANES

                def body(r, accs):
                    wb = plsc.load_gather(w_v, [zero_i + (g * PEER_CH + r)])
                    return tuple(
                        accs[j] + rows[r, pl.ds(base + j * SC_LANES, SC_LANES)] * wb
                        for j in range(CG))
                init = tuple(y_v[tl, pl.ds(base + j * SC_LANES, SC_LANES)] for j in range(CG))
                accs = lax.fori_loop(0, PEER_CH, body, init)
                for j in range(CG):
                    y_v[tl, pl.ds(base + j * SC_LANES, SC_LANES)] = accs[j]

        @pl.loop(0, tpw // tb)
        def _(blk):
            t0 = tok0 + blk * tb
            pltpu.sync_copy(idx_hbm.at[pl.ds(t0 * nch, tb * nch)], idx_v)
            pltpu.sync_copy(w_hbm.at[pl.ds(t0 * PEER_E, tb * PEER_E)], w_v)
            pltpu.sync_copy(x_hbm.at[pl.ds(t0, tb)], y_v)
            gather(0, rows0, sem0).start()

            @pl.loop(0, tb * nch, step=2)
            def _(g):
                gather(g + 1, rows1, sem1).start()
                gather(g, rows0, sem0).wait()
                compute(g, rows0)

                @pl.when(g + 2 < tb * nch)
                def _():
                    gather(g + 2, rows0, sem0).start()

                gather(g + 1, rows1, sem1).wait()
                compute(g + 1, rows1)

            pltpu.sync_copy(y_v, out_hbm.at[pl.ds(t0, tb)])

    return k(x2, w.reshape(-1), v, idx2)


def _weight_kernel(g_ref, a_ref, w_ref):
    a = a_ref[...]
    w_ref[...] = g_ref[...] * (0.5 * a * (1.0 + lax.erf(a * (1.0 / math.sqrt(2.0)))))


def _peer_weight(g, a, tm):
    T = g.shape[0]
    row = lambda i: (i, 0)
    return pl.pallas_call(
        _weight_kernel,
        grid=(T // tm,),
        in_specs=[pl.BlockSpec((tm, PEER_E), row)] * 2,
        out_specs=pl.BlockSpec((tm, PEER_E), row),
        out_shape=jax.ShapeDtypeStruct((T, PEER_E), F32),
        compiler_params=_cparams(("parallel",)),
        name="peer_weight",
    )(g, a)


def _tail(x2d, ret_o, rg, moba_o, memattn, weights, tm, sc_tb):
    (gn_g, gn_b, w_out_bf, norm2_g, w_mq_bf, mem_qn512, ones64, w_mo_bf, norm3_g, wq_bf, sk_hi, sk_lo, u, v) = weights
    x1, qm = _mix(x2d, ret_o, rg, moba_o, gn_g, gn_b, w_out_bf, norm2_g, w_mq_bf, mem_qn512, ones64, tm)
    om = memattn(qm)
    x2, h3, eidx, gate = _route(x1, om, w_mo_bf, norm3_g, wq_bf, sk_hi, sk_lo, tm)
    a = _peer_dot_sc(h3, u, eidx, sc_tb)
    w = _peer_weight(gate, a, tm)
    return _peer_sum_sc(x2, w, v, eidx, sc_tb)


def kernel(x_prompt, x_sample, state_ret, cache_moba_k, cache_moba_v, page_table, cache_mem_k, cache_mem_v, mem_prompt, norm1_g, w_in, ret_gn_g, ret_gn_b, moba_qn_g, moba_kn_g, w_out, norm2_g, mem_norm_g, w_mq, w_mkv, mem_qn_g, mem_kn_g, w_mo, norm3_g, peer_wq, peer_sub_keys, peer_u, peer_v):
    n_b, seq, _ = x_prompt.shape
    n_d, t_dec, _ = x_sample.shape
    past_len = page_table.shape[1] * PAGE_SIZE
    tm = 256

    w_in_bf = w_in.astype(BF16)
    w_out_bf = w_out.astype(BF16)
    w_mq_bf = w_mq.astype(BF16)
    w_mkv_bf = w_mkv.astype(BF16)
    w_mo_bf = w_mo.astype(BF16)
    wq_bf = peer_wq.astype(BF16)
    sk = peer_sub_keys.reshape(2 * PEER_HEADS, PEER_NKEYS, PEER_DQ // 2)
    sk_hi = sk.astype(BF16)
    sk_lo = (sk - sk_hi.astype(F32)).astype(BF16)
    ones64 = _block_ones(512, 64)
    tile8 = lambda g: jnp.tile(g, H_MOBA).reshape(1, 512)
    tile4 = lambda g: jnp.tile(g, H_MEM).reshape(1, 512)
    tail_w = (ret_gn_g, ret_gn_b, w_out_bf, norm2_g, w_mq_bf, tile4(mem_qn_g), ones64, w_mo_bf, norm3_g, wq_bf,
              sk_hi, sk_lo, peer_u, peer_v)

    xp = x_prompt.reshape(n_b * seq, D_MODEL)
    rq, rk, rv, rg, mq, mk, mv = _proj(xp, norm1_g, w_in_bf, tile8(moba_qn_g), tile8(moba_kn_g), ones64, tm)
    ret_o, st_p = _retention(rq, rk, rv, jnp.zeros((n_b, 4, LANES, LANES), F32), RET_CHUNK, RET_CHUNK,
                             n_b, seq // RET_CHUNK)
    moba_o = _moba_prompt(mq, mk, mv, n_b, seq)
    mem_k, mem_v = _memkv(mem_prompt.reshape(n_b * N_MEM, D_MODEL), mem_norm_g, w_mkv_bf, tile4(mem_kn_g), tm)
    y_p = _tail(xp, ret_o, rg, moba_o, lambda qm: _memattn_prompt(qm, mem_k, mem_v, n_b, seq, 512), tail_w, tm, 8)

    xs = x_sample.reshape(n_d * t_dec, D_MODEL)
    srq, srk, srv, srg, smq, smk, smv = _proj(xs, norm1_g, w_in_bf, tile8(moba_qn_g), tile8(moba_kn_g), ones64, tm)
    rows = 16
    pad = lambda a: jnp.pad(a.reshape(n_d, t_dec, 512), ((0, 0), (0, rows - t_dec), (0, 0))).reshape(n_d * rows, 512)
    sret_o, st_s = _retention(pad(srq), pad(srk), pad(srv), _to_block_diag(state_ret), t_dec, rows, n_d, 1)
    sret_o = sret_o.reshape(n_d, rows, 512)[:, :t_dec].reshape(n_d * t_dec, 512)
    smoba_o = _moba_sample(smq, smk, smv, cache_moba_k, cache_moba_v, page_table, past_len)
    y_s = _tail(xs, sret_o, srg, smoba_o,
                lambda qm: _memattn_sample(qm, cache_mem_k, cache_mem_v, n_d, t_dec), tail_w, tm, 8)

    return (y_p.reshape(n_b, seq, D_MODEL), y_s.reshape(n_d, t_dec, D_MODEL),
            _from_block_diag(st_p),
            mk.reshape(n_b, seq, H_MOBA, HD_MOBA), mv.reshape(n_b, seq, H_MOBA, HD_MOBA),
            mem_k.reshape(n_b, N_MEM, H_MEM, HD_MEM), mem_v.reshape(n_b, N_MEM, H_MEM, HD_MEM),
            _from_block_diag(st_s),
            smk.reshape(n_d, t_dec, H_MOBA, HD_MOBA), smv.reshape(n_d, t_dec, H_MOBA, HD_MOBA))
```

```python
import functools
import math

import jax
import jax.numpy as jnp
from jax import lax
from jax.experimental import pallas as pl
from jax.experimental.pallas import tpu as pltpu
from jax.experimental.pallas import tpu_sc as plsc

F32 = jnp.float32
BF16 = jnp.bfloat16
I32 = jnp.int32

D_MODEL = 1024
H_RET = 8
DK_RET = 64
RET_CHUNK = 128
RET_W = 512
H_MOBA = 8
HD_MOBA = 64
MOBA_BLOCK = 256
MOBA_TOPK = 3
MOBA_W = 512
IN_W = 3584
N_MEM = 256
H_MEM = 4
HD_MEM = 128
MEM_W = 512
PAGE_SIZE = 128
PEER_HEADS = 8
PEER_NKEYS = 128
PEER_DQ = 256
PEER_TOPK = 16
PEER_E = PEER_HEADS * PEER_TOPK
RMS_EPS = 1e-6
GN_EPS = 1e-5

LANES = 128
SC_CORES = 2
SC_SUBCORES = 16
SC_LANES = 16
SC_WORKERS = SC_CORES * SC_SUBCORES
VMEM_LIMIT = 56 * 1024 * 1024

MOBA_SAMPLE_PPS = 8
PROMPT_CHUNKS = 4

NEG = -1e30
NT_DIMS = (((1,), (1,)), ((), ()))


def _split_hi_lo(x):
    hi = x.astype(BF16)
    lo = (x - hi.astype(F32)).astype(BF16)
    return hi, lo


def _nt(a, b):
    return lax.dot_general(a, b, NT_DIMS, preferred_element_type=F32)


def _nt3(a, b):
    ah, al = _split_hi_lo(a)
    bh, bl = _split_hi_lo(b)
    return _nt(ah, bh) + _nt(ah, bl) + _nt(al, bh)


def _group_sum(x, ones_bd):
    hi, lo = _split_hi_lo(x)
    return (jnp.dot(hi, ones_bd, preferred_element_type=F32)
            + jnp.dot(lo, ones_bd, preferred_element_type=F32))


def _rms_rows(x, g):
    return x * lax.rsqrt(jnp.mean(x * x, axis=-1, keepdims=True) + RMS_EPS) * g


def _block_ones(width, group):
    i = jnp.arange(width) // group
    return (i[:, None] == i[None, :]).astype(BF16)


def _cparams(sem, vmem=None):
    return pltpu.CompilerParams(dimension_semantics=sem, vmem_limit_bytes=vmem)


def _proj_kernel(x_ref, g1_ref, w_ref, qn_ref, kn_ref, ones_ref,
                 rq_ref, rk_ref, rv_ref, rg_ref, mq_ref, mk_ref, mv_ref):
    xn = _rms_rows(x_ref[...], g1_ref[...]).astype(BF16)

    def col(i):
        return jnp.dot(xn, w_ref[:, i * 512:(i + 1) * 512], preferred_element_type=F32)

    rq_ref[...] = col(0)
    rk_ref[...] = col(1) * (DK_RET ** -0.5)
    rv_ref[...] = col(2)
    rg_ref[...] = col(3)
    ones = ones_ref[...]

    def headnorm(a, g):
        ms = _group_sum(a * a, ones) * (1.0 / HD_MOBA)
        return a * lax.rsqrt(ms + RMS_EPS) * g

    mq_ref[...] = headnorm(col(4), qn_ref[...]) * (HD_MOBA ** -0.5)
    mk_ref[...] = headnorm(col(5), kn_ref[...])
    mv_ref[...] = col(6)


def _proj(x2d, norm1_g, w_in_bf, qn512, kn512, ones64, tm):
    T = x2d.shape[0]
    out = jax.ShapeDtypeStruct((T, 512), F32)
    row = lambda i: (i, 0)
    const = lambda i: (0, 0)
    return pl.pallas_call(
        _proj_kernel,
        grid=(T // tm,),
        in_specs=[pl.BlockSpec((tm, D_MODEL), row), pl.BlockSpec((1, D_MODEL), const),
                  pl.BlockSpec((D_MODEL, IN_W), const), pl.BlockSpec((1, 512), const),
                  pl.BlockSpec((1, 512), const), pl.BlockSpec((512, 512), const)],
        out_specs=[pl.BlockSpec((tm, 512), row)] * 7,
        out_shape=[out] * 7,
        compiler_params=_cparams(("parallel",), VMEM_LIMIT),
        name="proj",
    )(x2d, norm1_g.reshape(1, -1), w_in_bf, qn512, kn512, ones64)


def _ret_kernel(q_ref, k_ref, v_ref, s0_ref, dec_ref, rowdec_ref, kdec_ref, sdec_ref, o_ref, st_ref, state_scr):
    c = pl.program_id(1)
    R = q_ref.shape[0]

    @pl.when(c == 0)
    def _():
        state_scr[...] = s0_ref[0]

    lane = lax.broadcasted_iota(I32, (R, LANES), 1)
    lo_lane = lane < 64
    bd = ((lax.broadcasted_iota(I32, (LANES, LANES), 0) < 64)
          == (lax.broadcasted_iota(I32, (LANES, LANES), 1) < 64))
    for p in range(4):
        sl = slice(p * LANES, (p + 1) * LANES)
        q2 = q_ref[:, sl]
        k2 = k_ref[:, sl]
        v2b = v_ref[:, sl].astype(BF16)
        k2b = k2.astype(BF16)
        qa = jnp.where(lo_lane, q2, 0.0).astype(BF16)
        qb = jnp.where(lo_lane, 0.0, q2).astype(BF16)
        sa = _nt(qa, k2b) * dec_ref[2 * p]
        sb = _nt(qb, k2b) * dec_ref[2 * p + 1]
        ia = jnp.dot(sa.astype(BF16), v2b, preferred_element_type=F32)
        ib = jnp.dot(sb.astype(BF16), v2b, preferred_element_type=F32)
        st = state_scr[p]
        cross = jnp.dot(q2.astype(BF16), st.astype(BF16), preferred_element_type=F32) * rowdec_ref[p]
        o_ref[:, sl] = jnp.where(lo_lane, ia, ib) + cross
        kdt = (k2 * kdec_ref[p]).T.astype(BF16)
        upd = jnp.dot(kdt, v2b, preferred_element_type=F32)
        state_scr[p] = jnp.where(bd, st * sdec_ref[p] + upd, 0.0)

    @pl.when(c == pl.num_programs(1) - 1)
    def _():
        st_ref[0] = state_scr[...]


def _ret_tables(t_real, rows):
    h = jnp.arange(H_RET, dtype=F32)
    log_g = jnp.log1p(-jnp.exp2(-5.0 - h))
    pos = jnp.arange(rows, dtype=F32)
    diff = pos[:, None] - pos[None, :]
    ok = (diff >= 0) & (pos[:, None] < t_real) & (pos[None, :] < t_real)
    dec = jnp.where(ok[None], jnp.exp(log_g[:, None, None] * jnp.where(ok, diff, 0.0)[None]), 0.0)
    lg_lane = jnp.repeat(log_g, 64).reshape(4, 1, LANES)
    rowdec = jnp.exp(lg_lane * (pos[None, :, None] + 1.0))
    kdec = jnp.exp(lg_lane * jnp.maximum(t_real - 1.0 - pos, 0.0)[None, :, None])
    sdec = jnp.exp(lg_lane * float(t_real))
    return dec, rowdec, kdec, sdec


def _to_block_diag(state):
    n = state.shape[0]
    s = state.reshape(n, 4, 2, 64, 64)
    z = jnp.zeros_like(s[:, :, 0])
    top = jnp.concatenate([s[:, :, 0], z], axis=-1)
    bot = jnp.concatenate([z, s[:, :, 1]], axis=-1)
    return jnp.concatenate([top, bot], axis=-2)


def _from_block_diag(sbd):
    n = sbd.shape[0]
    a = sbd[:, :, :64, :64]
    b = sbd[:, :, 64:, 64:]
    return jnp.stack([a, b], axis=2).reshape(n, H_RET, 64, 64)


def _retention(q, k, v, state_bd, t_real, rows, n_seq, n_chunks):
    dec, rowdec, kdec, sdec = _ret_tables(t_real, rows)
    T = q.shape[0]
    tok = lambda s, c: (s * n_chunks + c, 0)
    c3 = lambda s, c: (0, 0, 0)
    st = lambda s, c: (s, 0, 0, 0)
    return pl.pallas_call(
        _ret_kernel,
        grid=(n_seq, n_chunks),
        in_specs=[pl.BlockSpec((rows, 512), tok)] * 3 + [
            pl.BlockSpec((1, 4, LANES, LANES), st),
            pl.BlockSpec((H_RET, rows, rows), c3), pl.BlockSpec((4, rows, LANES), c3),
            pl.BlockSpec((4, rows, LANES), c3), pl.BlockSpec((4, 1, LANES), c3)],
        out_specs=[pl.BlockSpec((rows, 512), tok), pl.BlockSpec((1, 4, LANES, LANES), st)],
        out_shape=[jax.ShapeDtypeStruct((T, 512), F32), jax.ShapeDtypeStruct((n_seq, 4, LANES, LANES), F32)],
        scratch_shapes=[pltpu.VMEM((4, LANES, LANES), F32)],
        compiler_params=_cparams(("parallel", "arbitrary")),
        name="retention",
    )(q, k, v, state_bd, dec, rowdec, kdec, sdec)


def _moba_prompt_kernel(q_ref, k_ref, v_ref, alibi_ref, slope_ref, o_ref,
                        kb_scr, vt_scr, km_scr, bias_scr, m_scr, l_scr, acc_scr, o0_scr):
    i = pl.program_id(2)
    nb = km_scr.shape[0]
    B = MOBA_BLOCK

    @pl.when(i == 0)
    def _():
        for j in range(nb):
            kj = k_ref[j * B:(j + 1) * B, :]
            kb_scr[j * B:(j + 1) * B, :] = kj.astype(BF16)
            km_scr[j:j + 1, :] = jnp.sum(kj, axis=0, keepdims=True) * (1.0 / B)
            vt_scr[:, j * B:(j + 1) * B] = v_ref[j * B:(j + 1) * B, :].T.astype(BF16)

    q2 = q_ref[...]
    lane = lax.broadcasted_iota(I32, (B, LANES), 1)
    blk = lax.broadcasted_iota(I32, (nb, B), 0)
    key_r = lax.broadcasted_iota(I32, (B, B), 0)
    qry_c = lax.broadcasted_iota(I32, (B, B), 1)
    km = km_scr[...]

    for head in range(2):
        qh = jnp.where((lane < 64) == (head == 0), q2, 0.0)
        qhb = qh.astype(BF16)
        g = _nt3(km, qh)
        past = blk < i
        g = jnp.where(past, g, -jnp.inf)
        rank = jnp.zeros((nb, B), I32)
        for jj in range(nb):
            gj = g[jj:jj + 1, :]
            beats = (gj > g) | ((gj == g) & (jj < blk))
            rank = rank + beats.astype(I32)
        bias_scr[...] = jnp.where(past & (rank < MOBA_TOPK), 0.0, NEG)

        alibi = alibi_ref[0, head]
        slope_b = slope_ref[0, head]

        s = _nt(kb_scr[pl.ds(pl.multiple_of(i * B, B), B), :], qhb) - alibi
        s = jnp.where(key_r <= qry_c, s, NEG)
        m0 = jnp.max(s, axis=0, keepdims=True)
        p0 = jnp.exp(s - m0)
        m_scr[...] = m0
        l_scr[...] = jnp.sum(p0, axis=0, keepdims=True)
        acc_scr[...] = jnp.dot(vt_scr[:, pl.ds(pl.multiple_of(i * B, B), B)], p0.astype(BF16),
                               preferred_element_type=F32)

        def body(j, carry):
            off = pl.multiple_of(j * B, B)
            rowb = bias_scr[pl.ds(j, 1), :] - slope_b * (i - j).astype(F32)
            sj = _nt(kb_scr[pl.ds(off, B), :], qhb) - alibi + rowb
            m_old = m_scr[...]
            m_new = jnp.maximum(m_old, jnp.max(sj, axis=0, keepdims=True))
            alpha = jnp.exp(m_old - m_new)
            pj = jnp.exp(sj - m_new)
            l_scr[...] = alpha * l_scr[...] + jnp.sum(pj, axis=0, keepdims=True)
            acc_scr[...] = alpha * acc_scr[...] + jnp.dot(vt_scr[:, pl.ds(off, B)], pj.astype(BF16),
                                                          preferred_element_type=F32)
            m_scr[...] = m_new
            return carry

        lax.fori_loop(0, i, body, 0)
        res = acc_scr[...] * (1.0 / l_scr[...])
        if head == 0:
            o0_scr[...] = res
        else:
            row = lax.broadcasted_iota(I32, (LANES, B), 0)
            o_ref[...] = jnp.where(row < 64, o0_scr[...], res).T


def _moba_tables():
    slopes = jnp.exp2(-8.0 * (jnp.arange(H_MOBA, dtype=F32) + 1.0) / H_MOBA)
    r = jnp.arange(MOBA_BLOCK, dtype=F32)
    rel = r[None, :] - r[:, None]
    alibi = (slopes[:, None, None] * rel[None]).reshape(4, 2, MOBA_BLOCK, MOBA_BLOCK)
    slope_b = jnp.broadcast_to((slopes * MOBA_BLOCK)[:, None, None], (H_MOBA, 1, MOBA_BLOCK))
    return slopes, alibi, slope_b.reshape(4, 2, 1, MOBA_BLOCK)


def _moba_prompt(mq, mk, mv, n_seq, seq):
    _, alibi, slope_b = _moba_tables()
    nq = seq // MOBA_BLOCK
    T = mq.shape[0]
    B = MOBA_BLOCK
    return pl.pallas_call(
        _moba_prompt_kernel,
        grid=(n_seq, 4, nq),
        in_specs=[pl.BlockSpec((B, LANES), lambda b, p, i: (b * nq + i, p)),
                  pl.BlockSpec((seq, LANES), lambda b, p, i: (b, p)),
                  pl.BlockSpec((seq, LANES), lambda b, p, i: (b, p)),
                  pl.BlockSpec((1, 2, B, B), lambda b, p, i: (p, 0, 0, 0)),
                  pl.BlockSpec((1, 2, 1, B), lambda b, p, i: (p, 0, 0, 0))],
        out_specs=pl.BlockSpec((B, LANES), lambda b, p, i: (b * nq + i, p)),
        out_shape=jax.ShapeDtypeStruct((T, 512), F32),
        scratch_shapes=[pltpu.VMEM((seq, LANES), BF16), pltpu.VMEM((LANES, seq), BF16),
                        pltpu.VMEM((nq, LANES), F32), pltpu.VMEM((nq, B), F32),
                        pltpu.VMEM((1, B), F32), pltpu.VMEM((1, B), F32),
                        pltpu.VMEM((LANES, B), F32), pltpu.VMEM((LANES, B), F32)],
        compiler_params=_cparams(("parallel", "parallel", "arbitrary")),
        name="moba_prompt",
    )(mq, mk, mv, alibi, slope_b)


def _moba_sample_kernel(pt_ref, q_ref, kn_ref, vn_ref, *rest, past_len, pps):
    k_refs, v_refs = rest[:pps], rest[pps:2 * pps]
    (alibi_ref, slopec_ref, hm_ref, trow_ref, o_ref,
     qbd_scr, knew_scr, vnew_scr, m_scr, l_scr, g_scr, o_scr) = rest[2 * pps:]
    step = pl.program_id(1)
    R = qbd_scr.shape[0]
    lanei = lax.broadcasted_iota(I32, (R, LANES), 1)

    @pl.when(step == 0)
    def _():
        q = q_ref[0]
        hm = hm_ref[...]
        for t in range(4):
            qbd_scr[t * 8:(t + 1) * 8, :] = jnp.broadcast_to(q[t:t + 1, :], (8, 512)) * hm
        knew_scr[...] = jnp.zeros_like(knew_scr)
        vnew_scr[...] = jnp.zeros_like(vnew_scr)
        knew_scr[0:4, :] = kn_ref[0]
        vnew_scr[0:4, :] = vn_ref[0]
        m_scr[...] = jnp.zeros_like(m_scr)
        l_scr[...] = jnp.zeros_like(l_scr)
        g_scr[...] = jnp.zeros_like(g_scr)

    qbd_b = qbd_scr[...].astype(BF16)
    alibi = alibi_ref[...]
    slopec = slopec_ref[...]
    m_all = m_scr[...]
    l_all = l_scr[...]
    g_all = g_scr[...]
    raws = [jnp.dot(qbd_b, k_refs[j][0].astype(BF16), preferred_element_type=F32) for j in range(pps)]
    es = []
    for j in range(pps):
        p = step * pps + j
        off = (past_len - p * PAGE_SIZE).astype(F32)
        s = raws[j] - alibi - slopec * off
        m = jnp.max(s, axis=1, keepdims=True)
        e = jnp.exp(s - m)
        here = lanei == p
        g_all = jnp.where(here, jnp.sum(raws[j], axis=1, keepdims=True), g_all)
        m_all = jnp.where(here, m, m_all)
        l_all = jnp.where(here, jnp.sum(e, axis=1, keepdims=True), l_all)
        es.append(e.astype(BF16))
    for j in range(pps):
        o_scr[step * pps + j] = _nt(es[j], v_refs[j][0].astype(BF16))
    m_scr[...] = m_all
    l_scr[...] = l_all
    g_scr[...] = g_all

    @pl.when(step == pl.num_programs(1) - 1)
    def _():
        n_pg = past_len // PAGE_SIZE
        g = (g_all + pltpu.roll(g_all, LANES - 1, 1)) * (1.0 / MOBA_BLOCK)
        g = jnp.where((lanei < n_pg) & ((lanei & 1) == 0), g, -jnp.inf)
        sel = lanei < 0
        for _ in range(MOBA_TOPK):
            mx = jnp.max(g, axis=1, keepdims=True)
            ix = jnp.min(jnp.where(g == mx, lanei, LANES), axis=1, keepdims=True)
            pick = (lanei == ix) | (lanei == ix + 1)
            sel = sel | pick
            g = jnp.where(pick, -jnp.inf, g)
        sn = _nt(qbd_b, knew_scr[...].astype(BF16)) - alibi_ref[...]
        sn = jnp.where(lanei <= trow_ref[...], sn, NEG)
        mn = jnp.max(sn, axis=1, keepdims=True)
        en = jnp.exp(sn - mn)
        ln = jnp.sum(en, axis=1, keepdims=True)
        on = jnp.dot(en.astype(BF16), vnew_scr[...].astype(BF16), preferred_element_type=F32)
        m_all = m_scr[...]
        mt = jnp.maximum(jnp.max(jnp.where(sel, m_all, -jnp.inf), axis=1, keepdims=True), mn)
        w_all = jnp.where(sel, jnp.exp(m_all - mt), 0.0)
        wn = jnp.exp(mn - mt)
        den = jnp.sum(w_all * l_scr[...], axis=1, keepdims=True) + wn * ln
        num = wn * on
        for pg in range(n_pg):
            num = num + w_all[:, pg:pg + 1] * o_scr[pg]
        of = num * (1.0 / den)
        o_ref[0] = jnp.sum(of.reshape(4, 8, 512) * hm_ref[...][None], axis=1)


def _moba_sample(mq, mk, mv, cache_k, cache_v, page_table, past_len):
    n_seq, n_pages = page_table.shape
    slopes = jnp.exp2(-8.0 * (jnp.arange(H_MOBA, dtype=F32) + 1.0) / H_MOBA)
    row = jnp.arange(32)
    t_row = (row // 8).astype(F32)
    s_row = slopes[row % 8]
    lane = jnp.arange(LANES, dtype=F32)
    alibi = s_row[:, None] * (t_row[:, None] - lane[None, :])
    slopec = jnp.broadcast_to(s_row[:, None], (32, LANES))
    hm = (jnp.arange(512)[None, :] // 64 == jnp.arange(8)[:, None]).astype(F32)
    trow = jnp.broadcast_to((row // 8).astype(I32)[:, None], (32, LANES))
    n_pool = cache_k.shape[0]
    ck = jnp.transpose(cache_k, (0, 2, 3, 1)).reshape(n_pool, 512, PAGE_SIZE)
    cv = jnp.transpose(cache_v, (0, 2, 3, 1)).reshape(n_pool, 512, PAGE_SIZE)
    pps = MOBA_SAMPLE_PPS
    seq3 = lambda b, s, pt: (b, 0, 0)
    c2 = lambda b, s, pt: (0, 0)
    page_specs = [pl.BlockSpec((1, 512, PAGE_SIZE), lambda b, s, pt, j=j: (pt[b * n_pages + s * pps + j], 0, 0))
                  for j in range(pps)]
    grid_spec = pltpu.PrefetchScalarGridSpec(
        num_scalar_prefetch=1,
        grid=(n_seq, n_pages // pps),
        in_specs=[pl.BlockSpec((1, 4, 512), seq3)] * 3 + page_specs + page_specs + [
            pl.BlockSpec((32, LANES), c2), pl.BlockSpec((32, LANES), c2), pl.BlockSpec((8, 512), c2),
            pl.BlockSpec((32, LANES), c2)],
        out_specs=pl.BlockSpec((1, 4, 512), seq3),
        scratch_shapes=[pltpu.VMEM((32, 512), F32),
                        pltpu.VMEM((LANES, 512), F32), pltpu.VMEM((LANES, 512), F32),
                        pltpu.VMEM((32, LANES), F32), pltpu.VMEM((32, LANES), F32), pltpu.VMEM((32, LANES), F32),
                        pltpu.VMEM((n_pages, 32, 512), F32)],
    )
    out = pl.pallas_call(
        functools.partial(_moba_sample_kernel, past_len=past_len, pps=pps),
        grid_spec=grid_spec,
        out_shape=jax.ShapeDtypeStruct((n_seq, 4, 512), F32),
        compiler_params=_cparams(("parallel", "arbitrary")),
        name="moba_sample",
    )(page_table.reshape(-1), mq.reshape(n_seq, 4, 512), mk.reshape(n_seq, 4, 512), mv.reshape(n_seq, 4, 512),
      *([ck] * pps), *([cv] * pps), alibi, slopec, hm, trow)
    return out.reshape(n_seq * 4, 512)


def _mix_kernel(x_ref, ro_ref, rg_ref, mo_ref, gng_ref, gnb_ref, wout_ref, n2_ref, wmq_ref, qn_ref, ones_ref,
                x1_ref, qm_ref):
    ones = ones_ref[...]
    ro = ro_ref[...]
    mu = _group_sum(ro, ones) * (1.0 / 64)
    d = ro - mu
    var = _group_sum(d * d, ones) * (1.0 / 64)
    gn = d * lax.rsqrt(var + GN_EPS)
    rg = rg_ref[...]
    ret_y = (gn * gng_ref[...] + gnb_ref[...]) * (rg * jax.nn.sigmoid(rg))
    y = (jnp.dot(ret_y.astype(BF16), wout_ref[0:512, :], preferred_element_type=F32)
         + jnp.dot(mo_ref[...].astype(BF16), wout_ref[512:1024, :], preferred_element_type=F32))
    x1 = x_ref[...] + y
    x1_ref[...] = x1
    h2 = _rms_rows(x1, n2_ref[...]).astype(BF16)
    qm = jnp.dot(h2, wmq_ref[...], preferred_element_type=F32)
    qn = qn_ref[...]
    for h in range(H_MEM):
        sl = slice(h * HD_MEM, (h + 1) * HD_MEM)
        qm_ref[:, sl] = _rms_rows(qm[:, sl], qn[:, sl]) * (HD_MEM ** -0.5)


def _mix(x2d, ret_o, rg, moba_o, gn_g, gn_b, w_out_bf, norm2_g, w_mq_bf, qn512, ones64, tm):
    T = x2d.shape[0]
    row = lambda i: (i, 0)
    const = lambda i: (0, 0)
    return pl.pallas_call(
        _mix_kernel,
        grid=(T // tm,),
        in_specs=[pl.BlockSpec((tm, D_MODEL), row), pl.BlockSpec((tm, 512), row), pl.BlockSpec((tm, 512), row),
                  pl.BlockSpec((tm, 512), row), pl.BlockSpec((1, 512), const), pl.BlockSpec((1, 512), const),
                  pl.BlockSpec((D_MODEL, D_MODEL), const), pl.BlockSpec((1, D_MODEL), const),
                  pl.BlockSpec((D_MODEL, MEM_W), const), pl.BlockSpec((1, MEM_W), const),
                  pl.BlockSpec((512, 512), const)],
        out_specs=[pl.BlockSpec((tm, D_MODEL), row), pl.BlockSpec((tm, MEM_W), row)],
        out_shape=[jax.ShapeDtypeStruct((T, D_MODEL), F32), jax.ShapeDtypeStruct((T, MEM_W), F32)],
        compiler_params=_cparams(("parallel",), VMEM_LIMIT),
        name="mix_out",
    )(x2d, ret_o, rg, moba_o, gn_g.reshape(1, -1), gn_b.reshape(1, -1), w_out_bf, norm2_g.reshape(1, -1),
      w_mq_bf, qn512, ones64)


def _memkv_kernel(m_ref, g_ref, w_ref, kn_ref, k_ref, v_ref):
    mn = _rms_rows(m_ref[...], g_ref[...]).astype(BF16)
    k = jnp.dot(mn, w_ref[:, 0:MEM_W], preferred_element_type=F32)
    kn = kn_ref[...]
    for h in range(H_MEM):
        sl = slice(h * HD_MEM, (h + 1) * HD_MEM)
        k_ref[:, sl] = _rms_rows(k[:, sl], kn[:, sl])
    v_ref[...] = jnp.dot(mn, w_ref[:, MEM_W:2 * MEM_W], preferred_element_type=F32)


def _memkv(mem2d, mem_norm_g, w_mkv_bf, kn512, tm):
    T = mem2d.shape[0]
    row = lambda i: (i, 0)
    const = lambda i: (0, 0)
    out = jax.ShapeDtypeStruct((T, MEM_W), F32)
    return pl.pallas_call(
        _memkv_kernel,
        grid=(T // tm,),
        in_specs=[pl.BlockSpec((tm, D_MODEL), row), pl.BlockSpec((1, D_MODEL), const),
                  pl.BlockSpec((D_MODEL, 2 * MEM_W), const), pl.BlockSpec((1, MEM_W), const)],
        out_specs=[pl.BlockSpec((tm, MEM_W), row)] * 2,
        out_shape=[out, out],
        compiler_params=_cparams(("parallel",)),
        name="mem_kv",
    )(mem2d, mem_norm_g.reshape(1, -1), w_mkv_bf, kn512)


def _memattn_body(q, k, v):
    outs = []
    for h in range(H_MEM):
        sl = slice(h * HD_MEM, (h + 1) * HD_MEM)
        s = _nt(q[:, sl].astype(BF16), k[:, sl].astype(BF16))
        m = jnp.max(s, axis=1, keepdims=True)
        e = jnp.exp(s - m)
        pr = e * (1.0 / jnp.sum(e, axis=1, keepdims=True))
        outs.append(jnp.dot(pr.astype(BF16), v[:, sl].astype(BF16), preferred_element_type=F32))
    return outs


def _memattn_prompt_kernel(q_ref, k_ref, v_ref, o_ref):
    outs = _memattn_body(q_ref[...], k_ref[...], v_ref[...])
    for h in range(H_MEM):
        o_ref[:, h * HD_MEM:(h + 1) * HD_MEM] = outs[h]


def _memattn_prompt(qm, mem_k, mem_v, n_seq, seq, tq):
    T = qm.shape[0]
    nq = seq // tq
    return pl.pallas_call(
        _memattn_prompt_kernel,
        grid=(n_seq, nq),
        in_specs=[pl.BlockSpec((tq, MEM_W), lambda b, i: (b * nq + i, 0)),
                  pl.BlockSpec((N_MEM, MEM_W), lambda b, i: (b, 0)),
                  pl.BlockSpec((N_MEM, MEM_W), lambda b, i: (b, 0))],
        out_specs=pl.BlockSpec((tq, MEM_W), lambda b, i: (b * nq + i, 0)),
        out_shape=jax.ShapeDtypeStruct((T, MEM_W), F32),
        compiler_params=_cparams(("parallel", "arbitrary")),
        name="memattn_prompt",
    )(qm, mem_k, mem_v)


def _memattn_sample_kernel(q_ref, k_ref, v_ref, o_ref):
    outs = _memattn_body(q_ref[0], k_ref[0], v_ref[0])
    for h in range(H_MEM):
        o_ref[0, :, h * HD_MEM:(h + 1) * HD_MEM] = outs[h]


def _memattn_sample(qm, cache_k, cache_v, n_seq, t_dec):
    pad_rows = 16
    q3 = jnp.pad(qm.reshape(n_seq, t_dec, MEM_W), ((0, 0), (0, pad_rows - t_dec), (0, 0)))
    ck = cache_k.reshape(n_seq, N_MEM, MEM_W)
    cv = cache_v.reshape(n_seq, N_MEM, MEM_W)
    b3 = lambda b: (b, 0, 0)
    out = pl.pallas_call(
        _memattn_sample_kernel,
        grid=(n_seq,),
        in_specs=[pl.BlockSpec((1, pad_rows, MEM_W), b3), pl.BlockSpec((1, N_MEM, MEM_W), b3),
                  pl.BlockSpec((1, N_MEM, MEM_W), b3)],
        out_specs=pl.BlockSpec((1, pad_rows, MEM_W), b3),
        out_shape=jax.ShapeDtypeStruct((n_seq, pad_rows, MEM_W), F32),
        compiler_params=_cparams(("parallel",)),
        name="memattn_sample",
    )(q3, ck, cv)
    return out[:, :t_dec].reshape(n_seq * t_dec, MEM_W)


def _topk_rows(s, k, payload=None):
    n = s.shape[0]
    rowf = lax.broadcasted_iota(I32, s.shape, 0).astype(F32)
    vals, idxs, pays = [], [], []
    for _ in range(k):
        mx = jnp.max(s, axis=0, keepdims=True)
        ix = jnp.min(jnp.where(s == mx, rowf, float(n)), axis=0, keepdims=True)
        hit = rowf == ix
        vals.append(mx)
        idxs.append(ix)
        if payload is not None:
            pays.append(jnp.sum(jnp.where(hit, payload, 0.0), axis=0, keepdims=True))
        s = jnp.where(hit, -jnp.inf, s)
    cat = lambda xs: jnp.concatenate(xs, axis=0)
    return cat(vals), cat(idxs), (cat(pays) if payload is not None else None)


def _route_kernel(x1_ref, om_ref, wmo_ref, n3_ref, wq_ref, skh_ref, skl_ref,
                  x2_ref, h3_ref, e_ref, g_ref, q_scr, et_scr, gt_scr):
    x2 = x1_ref[...] + jnp.dot(om_ref[...].astype(BF16), wmo_ref[...], preferred_element_type=F32)
    x2_ref[...] = x2
    h3 = _rms_rows(x2, n3_ref[...])
    h3_ref[...] = h3
    h3b = h3.astype(BF16)
    for j in range(2 * PEER_HEADS):
        q_scr[j] = jnp.dot(h3b, wq_ref[:, j * LANES:(j + 1) * LANES], preferred_element_type=F32)

    K = PEER_TOPK

    def head_body(h, carry):
        sv, si = [], []
        for c in range(2):
            q = q_scr[2 * h + c]
            qh, ql = _split_hi_lo(q)
            kh = skh_ref[2 * h + c]
            s = _nt(kh, qh) + _nt(kh, ql) + _nt(skl_ref[2 * h + c], qh)
            v, ix, _ = _topk_rows(s, K)
            sv.append(v)
            si.append(ix)
        cand = [sv[0][0:1] + sv[1]]
        cidx = [si[0][0:1] * PEER_NKEYS + si[1]]
        for a in range(1, 8):
            cand.append(sv[0][a:a + 1] + sv[1][0:8])
            cidx.append(si[0][a:a + 1] * PEER_NKEYS + si[1][0:8])
        cand.append(sv[0][8:16] + sv[1][0:1])
        cidx.append(si[0][8:16] * PEER_NKEYS + si[1][0:1])
        best, _, eidx = _topk_rows(jnp.concatenate(cand, axis=0), K, jnp.concatenate(cidx, axis=0))
        e = jnp.exp(best - best[0:1])
        gate = e * (1.0 / jnp.sum(e, axis=0, keepdims=True))
        r0 = pl.multiple_of(h * K, K)
        et_scr[pl.ds(r0, K), :] = eidx
        gt_scr[pl.ds(r0, K), :] = gate
        return carry

    lax.fori_loop(0, PEER_HEADS, head_body, 0)
    e_ref[...] = et_scr[...].T.astype(I32)
    g_ref[...] = gt_scr[...].T


def _route(x1, omem, w_mo_bf, norm3_g, wq_bf, sk_hi, sk_lo, tm):
    T = x1.shape[0]
    row = lambda i: (i, 0)
    const = lambda i: (0, 0)
    c3 = lambda i: (0, 0, 0)
    return pl.pallas_call(
        _route_kernel,
        grid=(T // tm,),
        in_specs=[pl.BlockSpec((tm, D_MODEL), row), pl.BlockSpec((tm, MEM_W), row),
                  pl.BlockSpec((MEM_W, D_MODEL), const), pl.BlockSpec((1, D_MODEL), const),
                  pl.BlockSpec((D_MODEL, PEER_HEADS * PEER_DQ), const),
                  pl.BlockSpec((2 * PEER_HEADS, PEER_NKEYS, LANES), c3),
                  pl.BlockSpec((2 * PEER_HEADS, PEER_NKEYS, LANES), c3)],
        out_specs=[pl.BlockSpec((tm, D_MODEL), row), pl.BlockSpec((tm, D_MODEL), row),
                   pl.BlockSpec((tm, PEER_E), row), pl.BlockSpec((tm, PEER_E), row)],
        out_shape=[jax.ShapeDtypeStruct((T, D_MODEL), F32), jax.ShapeDtypeStruct((T, D_MODEL), F32),
                   jax.ShapeDtypeStruct((T, PEER_E), I32), jax.ShapeDtypeStruct((T, PEER_E), F32)],
        scratch_shapes=[pltpu.VMEM((2 * PEER_HEADS, tm, LANES), F32), pltpu.VMEM((PEER_E, tm), F32),
                        pltpu.VMEM((PEER_E, tm), F32)],
        compiler_params=_cparams(("parallel",), VMEM_LIMIT),
        name="peer_route",
    )(x1, omem, w_mo_bf, norm3_g.reshape(1, -1), wq_bf, sk_hi, sk_lo)


PEER_CH = 32
PEER_SC_TB = 16


def _sc_mesh():
    return plsc.VectorSubcoreMesh(core_axis_name="c", subcore_axis_name="s")


def _peer_dot_sc(h3, u, eidx, tb):
    T = h3.shape[0]
    tpw = T // SC_WORKERS
    nch = PEER_E // PEER_CH
    idx2 = eidx.reshape(T * nch, PEER_CH)

    @functools.partial(
        pl.kernel, mesh=_sc_mesh(),
        out_type=jax.ShapeDtypeStruct((T * PEER_E,), F32),
        scratch_types=[
            pltpu.VMEM((tb * nch, PEER_CH), I32),
            pltpu.VMEM((tb, D_MODEL), F32),
            pltpu.VMEM((PEER_CH, D_MODEL), F32),
            pltpu.VMEM((PEER_CH, D_MODEL), F32),
            pltpu.VMEM((tb * PEER_E,), F32),
            pltpu.VMEM((SC_LANES * SC_LANES,), F32),
            pltpu.SemaphoreType.DMA,
            pltpu.SemaphoreType.DMA,
        ],
        compiler_params=pltpu.CompilerParams(needs_layout_passes=False),
        name="peer_dot_sc",
    )
    def k(h_hbm, u_hbm, idx_hbm, out_hbm, idx_v, h_v, rows0, rows1, a_v, accm, sem0, sem1):
        wid = lax.axis_index("s") * SC_CORES + lax.axis_index("c")
        tok0 = wid * tpw
        lanes = lax.iota(I32, SC_LANES)

        def gather(g, rows, sem):
            return pltpu.make_async_copy(u_hbm.at[idx_v.at[g]], rows, sem)

        def compute(g, rows):
            tl = g // nch
            for half in range(PEER_CH // SC_LANES):
                def body(c, accs):
                    hc = h_v[tl, pl.ds(c * SC_LANES, SC_LANES)]
                    return tuple(
                        accs[r] + rows[half * SC_LANES + r, pl.ds(c * SC_LANES, SC_LANES)] * hc
                        for r in range(SC_LANES))
                accs = lax.fori_loop(0, D_MODEL // SC_LANES, body,
                                     tuple(jnp.zeros((SC_LANES,), F32) for _ in range(SC_LANES)))
                for r in range(SC_LANES):
                    accm[pl.ds(r * SC_LANES, SC_LANES)] = accs[r]
                tot = jnp.zeros((SC_LANES,), F32)
                for l in range(SC_LANES):
                    tot = tot + plsc.load_gather(accm, [lanes * SC_LANES + l])
                a_v[pl.ds(g * PEER_CH + half * SC_LANES, SC_LANES)] = tot

        @pl.loop(0, tpw // tb)
        def _(blk):
            t0 = tok0 + blk * tb
            pltpu.sync_copy(idx_hbm.at[pl.ds(t0 * nch, tb * nch)], idx_v)
            pltpu.sync_copy(h_hbm.at[pl.ds(t0, tb)], h_v)
            gather(0, rows0, sem0).start()

            @pl.loop(0, tb * nch, step=2)
            def _(g):
                gather(g + 1, rows1, sem1).start()
                gather(g, rows0, sem0).wait()
                compute(g, rows0)

                @pl.when(g + 2 < tb * nch)
                def _():
                    gather(g + 2, rows0, sem0).start()

                gather(g + 1, rows1, sem1).wait()
                compute(g + 1, rows1)

            pltpu.sync_copy(a_v, out_hbm.at[pl.ds(t0 * PEER_E, tb * PEER_E)])

    return k(h3, u, idx2).reshape(T, PEER_E)


def _peer_sum_sc(x2, w, v, eidx, tb):
    T = x2.shape[0]
    tpw = T // SC_WORKERS
    nch = PEER_E // PEER_CH
    idx2 = eidx.reshape(T * nch, PEER_CH)
    CG = 16
    ncg = D_MODEL // (CG * SC_LANES)

    @functools.partial(
        pl.kernel, mesh=_sc_mesh(),
        out_type=jax.ShapeDtypeStruct((T, D_MODEL), F32),
        scratch_types=[
            pltpu.VMEM((tb * nch, PEER_CH), I32),
            pltpu.VMEM((tb * PEER_E,), F32),
            pltpu.VMEM((PEER_CH, D_MODEL), F32),
            pltpu.VMEM((PEER_CH, D_MODEL), F32),
            pltpu.VMEM((tb, D_MODEL), F32),
            pltpu.SemaphoreType.DMA,
            pltpu.SemaphoreType.DMA,
        ],
        compiler_params=pltpu.CompilerParams(needs_layout_passes=False),
        name="peer_sum_sc",
    )
    def k(x_hbm, w_hbm, v_hbm, idx_hbm, out_hbm, idx_v, w_v, rows0, rows1, y_v, sem0, sem1):
        wid = lax.axis_index("s") * SC_CORES + lax.axis_index("c")
        tok0 = wid * tpw
        zero_i = jnp.zeros((SC_LANES,), I32)

        def gather(g, rows, sem):
            return pltpu.make_async_copy(v_hbm.at[idx_v.at[g]], rows, sem)

        def compute(g, rows):
            tl = g // nch
            for cg in range(ncg):
                base = cg * CG * SC_LANES

                def body(r, accs):
                    wb = plsc.load_gather(w_v, [zero_i + (g * PEER_CH + r)])
                    return tuple(
                        accs[j] + rows[r, pl.ds(base + j * SC_LANES, SC_LANES)] * wb
                        for j in range(CG))
                init = tuple(y_v[tl, pl.ds(base + j * SC_LANES, SC_LANES)] for j in range(CG))
                accs = lax.fori_loop(0, PEER_CH, body, init)
                for j in range(CG):
                    y_v[tl, pl.ds(base + j * SC_LANES, SC_LANES)] = accs[j]

        @pl.loop(0, tpw // tb)
        def _(blk):
            t0 = tok0 + blk * tb
            pltpu.sync_copy(idx_hbm.at[pl.ds(t0 * nch, tb * nch)], idx_v)
            pltpu.sync_copy(w_hbm.at[pl.ds(t0 * PEER_E, tb * PEER_E)], w_v)
            pltpu.sync_copy(x_hbm.at[pl.ds(t0, tb)], y_v)
            gather(0, rows0, sem0).start()

            @pl.loop(0, tb * nch, step=2)
            def _(g):
                gather(g + 1, rows1, sem1).start()
                gather(g, rows0, sem0).wait()
                compute(g, rows0)

                @pl.when(g + 2 < tb * nch)
                def _():
                    gather(g + 2, rows0, sem0).start()

                gather(g + 1, rows1, sem1).wait()
                compute(g + 1, rows1)

            pltpu.sync_copy(y_v, out_hbm.at[pl.ds(t0, tb)])

    return k(x2, w.reshape(-1), v, idx2)


def _weight_kernel(g_ref, a_ref, w_ref):
    a = a_ref[...]
    w_ref[...] = g_ref[...] * (0.5 * a * (1.0 + lax.erf(a * (1.0 / math.sqrt(2.0)))))


def _peer_weight(g, a, tm):
    T = g.shape[0]
    row = lambda i: (i, 0)
    return pl.pallas_call(
        _weight_kernel,
        grid=(T // tm,),
        in_specs=[pl.BlockSpec((tm, PEER_E), row)] * 2,
        out_specs=pl.BlockSpec((tm, PEER_E), row),
        out_shape=jax.ShapeDtypeStruct((T, PEER_E), F32),
        compiler_params=_cparams(("parallel",)),
        name="peer_weight",
    )(g, a)


def _tail(x2d, ret_o, rg, moba_o, memattn, weights, tm, sc_tb):
    (gn_g, gn_b, w_out_bf, norm2_g, w_mq_bf, mem_qn512, ones64, w_mo_bf, norm3_g, wq_bf, sk_hi, sk_lo, u, v) = weights
    x1, qm = _mix(x2d, ret_o, rg, moba_o, gn_g, gn_b, w_out_bf, norm2_g, w_mq_bf, mem_qn512, ones64, tm)
    om = memattn(qm)
    x2, h3, eidx, gate = _route(x1, om, w_mo_bf, norm3_g, wq_bf, sk_hi, sk_lo, tm)
    a = _peer_dot_sc(h3, u, eidx, sc_tb)
    w = _peer_weight(gate, a, tm)
    return _peer_sum_sc(x2, w, v, eidx, sc_tb)


def kernel(x_prompt, x_sample, state_ret, cache_moba_k, cache_moba_v, page_table, cache_mem_k, cache_mem_v, mem_prompt, norm1_g, w_in, ret_gn_g, ret_gn_b, moba_qn_g, moba_kn_g, w_out, norm2_g, mem_norm_g, w_mq, w_mkv, mem_qn_g, mem_kn_g, w_mo, norm3_g, peer_wq, peer_sub_keys, peer_u, peer_v):
    n_b, seq, _ = x_prompt.shape
    n_d, t_dec, _ = x_sample.shape
    past_len = page_table.shape[1] * PAGE_SIZE
    tm = 256

    w_in_bf = w_in.astype(BF16)
    w_out_bf = w_out.astype(BF16)
    w_mq_bf = w_mq.astype(BF16)
    w_mkv_bf = w_mkv.astype(BF16)
    w_mo_bf = w_mo.astype(BF16)
    wq_bf = peer_wq.astype(BF16)
    sk = peer_sub_keys.reshape(2 * PEER_HEADS, PEER_NKEYS, PEER_DQ // 2)
    sk_hi = sk.astype(BF16)
    sk_lo = (sk - sk_hi.astype(F32)).astype(BF16)
    ones64 = _block_ones(512, 64)
    tile8 = lambda g: jnp.tile(g, H_MOBA).reshape(1, 512)
    tile4 = lambda g: jnp.tile(g, H_MEM).reshape(1, 512)
    tail_w = (ret_gn_g, ret_gn_b, w_out_bf, norm2_g, w_mq_bf, tile4(mem_qn_g), ones64, w_mo_bf, norm3_g, wq_bf,
              sk_hi, sk_lo, peer_u, peer_v)

    n_c = n_b // PROMPT_CHUNKS
    parts = []
    for c in range(PROMPT_CHUNKS):
        xp = x_prompt[c * n_c:(c + 1) * n_c].reshape(n_c * seq, D_MODEL)
        rq, rk, rv, rg, mq, mk, mv = _proj(xp, norm1_g, w_in_bf, tile8(moba_qn_g), tile8(moba_kn_g), ones64, tm)
        ret_o, st_p = _retention(rq, rk, rv, jnp.zeros((n_c, 4, LANES, LANES), F32), RET_CHUNK, RET_CHUNK,
                                 n_c, seq // RET_CHUNK)
        moba_o = _moba_prompt(mq, mk, mv, n_c, seq)
        mem_k, mem_v = _memkv(mem_prompt[c * n_c:(c + 1) * n_c].reshape(n_c * N_MEM, D_MODEL), mem_norm_g,
                              w_mkv_bf, tile4(mem_kn_g), tm)
        y_p = _tail(xp, ret_o, rg, moba_o,
                    functools.partial(_memattn_prompt, mem_k=mem_k, mem_v=mem_v, n_seq=n_c, seq=seq, tq=512),
                    tail_w, tm, PEER_SC_TB)
        parts.append((y_p, st_p, mk, mv, mem_k, mem_v))
    y_p, st_p, mk, mv, mem_k, mem_v = (jnp.concatenate(p, axis=0) for p in zip(*parts))

    xs = x_sample.reshape(n_d * t_dec, D_MODEL)
    srq, srk, srv, srg, smq, smk, smv = _proj(xs, norm1_g, w_in_bf, tile8(moba_qn_g), tile8(moba_kn_g), ones64, tm)
    rows = 16
    pad = lambda a: jnp.pad(a.reshape(n_d, t_dec, 512), ((0, 0), (0, rows - t_dec), (0, 0))).reshape(n_d * rows, 512)
    sret_o, st_s = _retention(pad(srq), pad(srk), pad(srv), _to_block_diag(state_ret), t_dec, rows, n_d, 1)
    sret_o = sret_o.reshape(n_d, rows, 512)[:, :t_dec].reshape(n_d * t_dec, 512)
    smoba_o = _moba_sample(smq, smk, smv, cache_moba_k, cache_moba_v, page_table, past_len)
    y_s = _tail(xs, sret_o, srg, smoba_o,
                lambda qm: _memattn_sample(qm, cache_mem_k, cache_mem_v, n_d, t_dec), tail_w, tm, PEER_SC_TB)

    return (y_p.reshape(n_b, seq, D_MODEL), y_s.reshape(n_d, t_dec, D_MODEL),
            _from_block_diag(st_p),
            mk.reshape(n_b, seq, H_MOBA, HD_MOBA), mv.reshape(n_b, seq, H_MOBA, HD_MOBA),
            mem_k.reshape(n_b, N_MEM, H_MEM, HD_MEM), mem_v.reshape(n_b, N_MEM, H_MEM, HD_MEM),
            _from_block_diag(st_s),
            smk.reshape(n_d, t_dec, H_MOBA, HD_MOBA), smv.reshape(n_d, t_dec, H_MOBA, HD_MOBA))
```

```python
import functools
import math

import jax
import jax.numpy as jnp
from jax import lax
from jax.experimental import pallas as pl
from jax.experimental.pallas import tpu as pltpu
from jax.experimental.pallas import tpu_sc as plsc

F32 = jnp.float32
BF16 = jnp.bfloat16
I32 = jnp.int32

D_MODEL = 1024
H_RET = 8
DK_RET = 64
RET_CHUNK = 128
RET_W = 512
H_MOBA = 8
HD_MOBA = 64
MOBA_BLOCK = 256
MOBA_TOPK = 3
MOBA_W = 512
IN_W = 3584
N_MEM = 256
H_MEM = 4
HD_MEM = 128
MEM_W = 512
PAGE_SIZE = 128
PEER_HEADS = 8
PEER_NKEYS = 128
PEER_DQ = 256
PEER_TOPK = 16
PEER_E = PEER_HEADS * PEER_TOPK
RMS_EPS = 1e-6
GN_EPS = 1e-5

LANES = 128
SC_CORES = 2
SC_SUBCORES = 16
SC_LANES = 16
SC_WORKERS = SC_CORES * SC_SUBCORES
VMEM_LIMIT = 56 * 1024 * 1024

MOBA_SAMPLE_PPS = 8
PROMPT_CHUNKS = 4

NEG = -1e30
NT_DIMS = (((1,), (1,)), ((), ()))


def _split_hi_lo(x):
    hi = x.astype(BF16)
    lo = (x - hi.astype(F32)).astype(BF16)
    return hi, lo


def _nt(a, b):
    return lax.dot_general(a, b, NT_DIMS, preferred_element_type=F32)


def _nt3(a, b):
    ah, al = _split_hi_lo(a)
    bh, bl = _split_hi_lo(b)
    return _nt(ah, bh) + _nt(ah, bl) + _nt(al, bh)


def _group_sum(x, ones_bd):
    hi, lo = _split_hi_lo(x)
    return (jnp.dot(hi, ones_bd, preferred_element_type=F32)
            + jnp.dot(lo, ones_bd, preferred_element_type=F32))


def _rms_rows(x, g):
    return x * lax.rsqrt(jnp.mean(x * x, axis=-1, keepdims=True) + RMS_EPS) * g


def _block_ones(width, group):
    i = jnp.arange(width) // group
    return (i[:, None] == i[None, :]).astype(BF16)


def _cparams(sem, vmem=None):
    return pltpu.CompilerParams(dimension_semantics=sem, vmem_limit_bytes=vmem)


def _proj_kernel(x_ref, g1_ref, w_ref, qn_ref, kn_ref, ones_ref,
                 rq_ref, rk_ref, rv_ref, rg_ref, mq_ref, mk_ref, mv_ref):
    xn = _rms_rows(x_ref[...], g1_ref[...]).astype(BF16)

    def col(i):
        return jnp.dot(xn, w_ref[:, i * 512:(i + 1) * 512], preferred_element_type=F32)

    rq_ref[...] = col(0)
    rk_ref[...] = col(1) * (DK_RET ** -0.5)
    rv_ref[...] = col(2)
    rg_ref[...] = col(3)
    ones = ones_ref[...]

    def headnorm(a, g):
        ms = _group_sum(a * a, ones) * (1.0 / HD_MOBA)
        return a * lax.rsqrt(ms + RMS_EPS) * g

    mq_ref[...] = headnorm(col(4), qn_ref[...]) * (HD_MOBA ** -0.5)
    mk_ref[...] = headnorm(col(5), kn_ref[...])
    mv_ref[...] = col(6)


def _proj(x2d, norm1_g, w_in_bf, qn512, kn512, ones64, tm):
    T = x2d.shape[0]
    out = jax.ShapeDtypeStruct((T, 512), F32)
    row = lambda i: (i, 0)
    const = lambda i: (0, 0)
    return pl.pallas_call(
        _proj_kernel,
        grid=(T // tm,),
        in_specs=[pl.BlockSpec((tm, D_MODEL), row), pl.BlockSpec((1, D_MODEL), const),
                  pl.BlockSpec((D_MODEL, IN_W), const), pl.BlockSpec((1, 512), const),
                  pl.BlockSpec((1, 512), const), pl.BlockSpec((512, 512), const)],
        out_specs=[pl.BlockSpec((tm, 512), row)] * 7,
        out_shape=[out] * 7,
        compiler_params=_cparams(("parallel",), VMEM_LIMIT),
        name="proj",
    )(x2d, norm1_g.reshape(1, -1), w_in_bf, qn512, kn512, ones64)


def _ret_kernel(q_ref, k_ref, v_ref, s0_ref, dec_ref, rowdec_ref, kdec_ref, sdec_ref, o_ref, st_ref, state_scr):
    c = pl.program_id(1)
    R = q_ref.shape[0]

    @pl.when(c == 0)
    def _():
        state_scr[...] = s0_ref[0]

    lane = lax.broadcasted_iota(I32, (R, LANES), 1)
    lo_lane = lane < 64
    bd = ((lax.broadcasted_iota(I32, (LANES, LANES), 0) < 64)
          == (lax.broadcasted_iota(I32, (LANES, LANES), 1) < 64))
    for p in range(4):
        sl = slice(p * LANES, (p + 1) * LANES)
        q2 = q_ref[:, sl]
        k2 = k_ref[:, sl]
        v2b = v_ref[:, sl].astype(BF16)
        k2b = k2.astype(BF16)
        qa = jnp.where(lo_lane, q2, 0.0).astype(BF16)
        qb = jnp.where(lo_lane, 0.0, q2).astype(BF16)
        sa = _nt(qa, k2b) * dec_ref[2 * p]
        sb = _nt(qb, k2b) * dec_ref[2 * p + 1]
        ia = jnp.dot(sa.astype(BF16), v2b, preferred_element_type=F32)
        ib = jnp.dot(sb.astype(BF16), v2b, preferred_element_type=F32)
        st = state_scr[p]
        cross = jnp.dot(q2.astype(BF16), st.astype(BF16), preferred_element_type=F32) * rowdec_ref[p]
        o_ref[:, sl] = jnp.where(lo_lane, ia, ib) + cross
        kdt = (k2 * kdec_ref[p]).T.astype(BF16)
        upd = jnp.dot(kdt, v2b, preferred_element_type=F32)
        state_scr[p] = jnp.where(bd, st * sdec_ref[p] + upd, 0.0)

    @pl.when(c == pl.num_programs(1) - 1)
    def _():
        st_ref[0] = state_scr[...]


def _ret_tables(t_real, rows):
    h = jnp.arange(H_RET, dtype=F32)
    log_g = jnp.log1p(-jnp.exp2(-5.0 - h))
    pos = jnp.arange(rows, dtype=F32)
    diff = pos[:, None] - pos[None, :]
    ok = (diff >= 0) & (pos[:, None] < t_real) & (pos[None, :] < t_real)
    dec = jnp.where(ok[None], jnp.exp(log_g[:, None, None] * jnp.where(ok, diff, 0.0)[None]), 0.0)
    lg_lane = jnp.repeat(log_g, 64).reshape(4, 1, LANES)
    rowdec = jnp.exp(lg_lane * (pos[None, :, None] + 1.0))
    kdec = jnp.exp(lg_lane * jnp.maximum(t_real - 1.0 - pos, 0.0)[None, :, None])
    sdec = jnp.exp(lg_lane * float(t_real))
    return dec, rowdec, kdec, sdec


def _to_block_diag(state):
    n = state.shape[0]
    s = state.reshape(n, 4, 2, 64, 64)
    z = jnp.zeros_like(s[:, :, 0])
    top = jnp.concatenate([s[:, :, 0], z], axis=-1)
    bot = jnp.concatenate([z, s[:, :, 1]], axis=-1)
    return jnp.concatenate([top, bot], axis=-2)


def _from_block_diag(sbd):
    n = sbd.shape[0]
    a = sbd[:, :, :64, :64]
    b = sbd[:, :, 64:, 64:]
    return jnp.stack([a, b], axis=2).reshape(n, H_RET, 64, 64)


def _retention(q, k, v, state_bd, t_real, rows, n_seq, n_chunks):
    dec, rowdec, kdec, sdec = _ret_tables(t_real, rows)
    T = q.shape[0]
    tok = lambda s, c: (s * n_chunks + c, 0)
    c3 = lambda s, c: (0, 0, 0)
    st = lambda s, c: (s, 0, 0, 0)
    return pl.pallas_call(
        _ret_kernel,
        grid=(n_seq, n_chunks),
        in_specs=[pl.BlockSpec((rows, 512), tok)] * 3 + [
            pl.BlockSpec((1, 4, LANES, LANES), st),
            pl.BlockSpec((H_RET, rows, rows), c3), pl.BlockSpec((4, rows, LANES), c3),
            pl.BlockSpec((4, rows, LANES), c3), pl.BlockSpec((4, 1, LANES), c3)],
        out_specs=[pl.BlockSpec((rows, 512), tok), pl.BlockSpec((1, 4, LANES, LANES), st)],
        out_shape=[jax.ShapeDtypeStruct((T, 512), F32), jax.ShapeDtypeStruct((n_seq, 4, LANES, LANES), F32)],
        scratch_shapes=[pltpu.VMEM((4, LANES, LANES), F32)],
        compiler_params=_cparams(("parallel", "arbitrary")),
        name="retention",
    )(q, k, v, state_bd, dec, rowdec, kdec, sdec)


def _moba_prompt_kernel(q_ref, k_ref, v_ref, alibi_ref, slope_ref, o_ref,
                        kb_scr, vt_scr, km_scr, bias_scr, m_scr, l_scr, acc_scr):
    i = pl.program_id(2)
    nb = km_scr.shape[0]
    B = MOBA_BLOCK

    @pl.when(i == 0)
    def _():
        for j in range(nb):
            kj = k_ref[j * B:(j + 1) * B, :]
            kb_scr[j * B:(j + 1) * B, :] = kj.astype(BF16)
            km_scr[j:j + 1, :] = jnp.sum(kj, axis=0, keepdims=True) * (1.0 / B)
            vt_scr[:, j * B:(j + 1) * B] = v_ref[j * B:(j + 1) * B, :].T.astype(BF16)

    q2 = q_ref[...]
    lane = lax.broadcasted_iota(I32, (B, LANES), 1)
    qq = jnp.concatenate([jnp.where(lane < 64, q2, 0.0), jnp.where(lane < 64, 0.0, q2)], axis=0)
    qqb = qq.astype(BF16)

    blk = lax.broadcasted_iota(I32, (nb, 2 * B), 0)
    past = blk < i
    g = jnp.where(past, _nt3(km_scr[...], qq), -jnp.inf)
    rank = jnp.zeros((nb, 2 * B), I32)
    for jj in range(nb):
        gj = g[jj:jj + 1, :]
        rank = rank + ((gj > g) | ((gj == g) & (jj < blk))).astype(I32)
    bias_scr[...] = jnp.where(past & (rank < MOBA_TOPK), 0.0, NEG)

    alibi = alibi_ref[0]
    slope_b = slope_ref[0]

    key_r = lax.broadcasted_iota(I32, (B, 2 * B), 0)
    qry_c = lax.broadcasted_iota(I32, (B, 2 * B), 1)
    qry_c = jnp.where(qry_c >= B, qry_c - B, qry_c)
    own = pl.multiple_of(i * B, B)
    s = jnp.where(key_r <= qry_c, _nt(kb_scr[pl.ds(own, B), :], qqb) - alibi, NEG)
    m0 = jnp.max(s, axis=0, keepdims=True)
    p0 = jnp.exp(s - m0)
    m_scr[...] = m0
    l_scr[...] = jnp.sum(p0, axis=0, keepdims=True)
    acc_scr[...] = jnp.dot(vt_scr[:, pl.ds(own, B)], p0.astype(BF16), preferred_element_type=F32)

    def body(j, carry):
        off = pl.multiple_of(j * B, B)
        rowb = bias_scr[pl.ds(j, 1), :] - slope_b * (i - j).astype(F32)
        sj = _nt(kb_scr[pl.ds(off, B), :], qqb) - alibi + rowb
        m_old = m_scr[...]
        m_new = jnp.maximum(m_old, jnp.max(sj, axis=0, keepdims=True))
        alpha = jnp.exp(m_old - m_new)
        pj = jnp.exp(sj - m_new)
        l_scr[...] = alpha * l_scr[...] + jnp.sum(pj, axis=0, keepdims=True)
        acc_scr[...] = alpha * acc_scr[...] + jnp.dot(vt_scr[:, pl.ds(off, B)], pj.astype(BF16),
                                                      preferred_element_type=F32)
        m_scr[...] = m_new
        return carry

    lax.fori_loop(0, i, body, 0)
    res = acc_scr[...] * (1.0 / l_scr[...])
    row = lax.broadcasted_iota(I32, (LANES, B), 0)
    o_ref[...] = jnp.where(row < 64, res[:, :B], res[:, B:]).T


def _moba_tables():
    slopes = jnp.exp2(-8.0 * (jnp.arange(H_MOBA, dtype=F32) + 1.0) / H_MOBA)
    r = jnp.arange(MOBA_BLOCK, dtype=F32)
    rel = r[None, :] - r[:, None]
    alibi = (slopes[:, None, None] * rel[None]).reshape(4, 2, MOBA_BLOCK, MOBA_BLOCK)
    alibi = jnp.concatenate([alibi[:, 0], alibi[:, 1]], axis=-1)
    slope_b = jnp.repeat((slopes * MOBA_BLOCK).reshape(4, 1, 2), MOBA_BLOCK, axis=-1)
    return slopes, alibi, slope_b


def _moba_prompt(mq, mk, mv, n_seq, seq):
    _, alibi, slope_b = _moba_tables()
    nq = seq // MOBA_BLOCK
    T = mq.shape[0]
    B = MOBA_BLOCK
    return pl.pallas_call(
        _moba_prompt_kernel,
        grid=(n_seq, 4, nq),
        in_specs=[pl.BlockSpec((B, LANES), lambda b, p, i: (b * nq + i, p)),
                  pl.BlockSpec((seq, LANES), lambda b, p, i: (b, p)),
                  pl.BlockSpec((seq, LANES), lambda b, p, i: (b, p)),
                  pl.BlockSpec((1, B, 2 * B), lambda b, p, i: (p, 0, 0)),
                  pl.BlockSpec((1, 1, 2 * B), lambda b, p, i: (p, 0, 0))],
        out_specs=pl.BlockSpec((B, LANES), lambda b, p, i: (b * nq + i, p)),
        out_shape=jax.ShapeDtypeStruct((T, 512), F32),
        scratch_shapes=[pltpu.VMEM((seq, LANES), BF16), pltpu.VMEM((LANES, seq), BF16),
                        pltpu.VMEM((nq, LANES), F32), pltpu.VMEM((nq, 2 * B), F32),
                        pltpu.VMEM((1, 2 * B), F32), pltpu.VMEM((1, 2 * B), F32),
                        pltpu.VMEM((LANES, 2 * B), F32)],
        compiler_params=_cparams(("parallel", "parallel", "arbitrary")),
        name="moba_prompt",
    )(mq, mk, mv, alibi, slope_b)


def _moba_sample_kernel(pt_ref, q_ref, kn_ref, vn_ref, *rest, past_len, pps):
    k_refs, v_refs = rest[:pps], rest[pps:2 * pps]
    (alibi_ref, slopec_ref, hm_ref, trow_ref, o_ref,
     qbd_scr, knew_scr, vnew_scr, m_scr, l_scr, g_scr, o_scr) = rest[2 * pps:]
    step = pl.program_id(1)
    R = qbd_scr.shape[0]
    lanei = lax.broadcasted_iota(I32, (R, LANES), 1)

    @pl.when(step == 0)
    def _():
        q = q_ref[0]
        hm = hm_ref[...]
        for t in range(4):
            qbd_scr[t * 8:(t + 1) * 8, :] = jnp.broadcast_to(q[t:t + 1, :], (8, 512)) * hm
        knew_scr[...] = jnp.zeros_like(knew_scr)
        vnew_scr[...] = jnp.zeros_like(vnew_scr)
        knew_scr[0:4, :] = kn_ref[0]
        vnew_scr[0:4, :] = vn_ref[0]
        m_scr[...] = jnp.zeros_like(m_scr)
        l_scr[...] = jnp.zeros_like(l_scr)
        g_scr[...] = jnp.zeros_like(g_scr)

    qbd_b = qbd_scr[...].astype(BF16)
    alibi = alibi_ref[...]
    slopec = slopec_ref[...]
    m_all = m_scr[...]
    l_all = l_scr[...]
    g_all = g_scr[...]
    raws = [jnp.dot(qbd_b, k_refs[j][0].astype(BF16), preferred_element_type=F32) for j in range(pps)]
    es = []
    for j in range(pps):
        p = step * pps + j
        off = (past_len - p * PAGE_SIZE).astype(F32)
        s = raws[j] - alibi - slopec * off
        m = jnp.max(s, axis=1, keepdims=True)
        e = jnp.exp(s - m)
        here = lanei == p
        g_all = jnp.where(here, jnp.sum(raws[j], axis=1, keepdims=True), g_all)
        m_all = jnp.where(here, m, m_all)
        l_all = jnp.where(here, jnp.sum(e, axis=1, keepdims=True), l_all)
        es.append(e.astype(BF16))
    for j in range(pps):
        o_scr[step * pps + j] = _nt(es[j], v_refs[j][0].astype(BF16))
    m_scr[...] = m_all
    l_scr[...] = l_all
    g_scr[...] = g_all

    @pl.when(step == pl.num_programs(1) - 1)
    def _():
        n_pg = past_len // PAGE_SIZE
        g = (g_all + pltpu.roll(g_all, LANES - 1, 1)) * (1.0 / MOBA_BLOCK)
        g = jnp.where((lanei < n_pg) & ((lanei & 1) == 0), g, -jnp.inf)
        sel = lanei < 0
        for _ in range(MOBA_TOPK):
            mx = jnp.max(g, axis=1, keepdims=True)
            ix = jnp.min(jnp.where(g == mx, lanei, LANES), axis=1, keepdims=True)
            pick = (lanei == ix) | (lanei == ix + 1)
            sel = sel | pick
            g = jnp.where(pick, -jnp.inf, g)
        sn = _nt(qbd_b, knew_scr[...].astype(BF16)) - alibi_ref[...]
        sn = jnp.where(lanei <= trow_ref[...], sn, NEG)
        mn = jnp.max(sn, axis=1, keepdims=True)
        en = jnp.exp(sn - mn)
        ln = jnp.sum(en, axis=1, keepdims=True)
        on = jnp.dot(en.astype(BF16), vnew_scr[...].astype(BF16), preferred_element_type=F32)
        m_all = m_scr[...]
        mt = jnp.maximum(jnp.max(jnp.where(sel, m_all, -jnp.inf), axis=1, keepdims=True), mn)
        w_all = jnp.where(sel, jnp.exp(m_all - mt), 0.0)
        wn = jnp.exp(mn - mt)
        den = jnp.sum(w_all * l_scr[...], axis=1, keepdims=True) + wn * ln
        num = wn * on
        for pg in range(n_pg):
            num = num + w_all[:, pg:pg + 1] * o_scr[pg]
        of = num * (1.0 / den)
        o_ref[0] = jnp.sum(of.reshape(4, 8, 512) * hm_ref[...][None], axis=1)


def _moba_sample(mq, mk, mv, cache_k, cache_v, page_table, past_len):
    n_seq, n_pages = page_table.shape
    slopes = jnp.exp2(-8.0 * (jnp.arange(H_MOBA, dtype=F32) + 1.0) / H_MOBA)
    row = jnp.arange(32)
    t_row = (row // 8).astype(F32)
    s_row = slopes[row % 8]
    lane = jnp.arange(LANES, dtype=F32)
    alibi = s_row[:, None] * (t_row[:, None] - lane[None, :])
    slopec = jnp.broadcast_to(s_row[:, None], (32, LANES))
    hm = (jnp.arange(512)[None, :] // 64 == jnp.arange(8)[:, None]).astype(F32)
    trow = jnp.broadcast_to((row // 8).astype(I32)[:, None], (32, LANES))
    n_pool = cache_k.shape[0]
    ck = jnp.transpose(cache_k, (0, 2, 3, 1)).reshape(n_pool, 512, PAGE_SIZE)
    cv = jnp.transpose(cache_v, (0, 2, 3, 1)).reshape(n_pool, 512, PAGE_SIZE)
    pps = MOBA_SAMPLE_PPS
    seq3 = lambda b, s, pt: (b, 0, 0)
    c2 = lambda b, s, pt: (0, 0)
    page_specs = [pl.BlockSpec((1, 512, PAGE_SIZE), lambda b, s, pt, j=j: (pt[b * n_pages + s * pps + j], 0, 0))
                  for j in range(pps)]
    grid_spec = pltpu.PrefetchScalarGridSpec(
        num_scalar_prefetch=1,
        grid=(n_seq, n_pages // pps),
        in_specs=[pl.BlockSpec((1, 4, 512), seq3)] * 3 + page_specs + page_specs + [
            pl.BlockSpec((32, LANES), c2), pl.BlockSpec((32, LANES), c2), pl.BlockSpec((8, 512), c2),
            pl.BlockSpec((32, LANES), c2)],
        out_specs=pl.BlockSpec((1, 4, 512), seq3),
        scratch_shapes=[pltpu.VMEM((32, 512), F32),
                        pltpu.VMEM((LANES, 512), F32), pltpu.VMEM((LANES, 512), F32),
                        pltpu.VMEM((32, LANES), F32), pltpu.VMEM((32, LANES), F32), pltpu.VMEM((32, LANES), F32),
                        pltpu.VMEM((n_pages, 32, 512), F32)],
    )
    out = pl.pallas_call(
        functools.partial(_moba_sample_kernel, past_len=past_len, pps=pps),
        grid_spec=grid_spec,
        out_shape=jax.ShapeDtypeStruct((n_seq, 4, 512), F32),
        compiler_params=_cparams(("parallel", "arbitrary")),
        name="moba_sample",
    )(page_table.reshape(-1), mq.reshape(n_seq, 4, 512), mk.reshape(n_seq, 4, 512), mv.reshape(n_seq, 4, 512),
      *([ck] * pps), *([cv] * pps), alibi, slopec, hm, trow)
    return out.reshape(n_seq * 4, 512)


def _mix_kernel(x_ref, ro_ref, rg_ref, mo_ref, gng_ref, gnb_ref, wout_ref, n2_ref, wmq_ref, qn_ref, ones_ref,
                x1_ref, qm_ref):
    ones = ones_ref[...]
    ro = ro_ref[...]
    mu = _group_sum(ro, ones) * (1.0 / 64)
    d = ro - mu
    var = _group_sum(d * d, ones) * (1.0 / 64)
    gn = d * lax.rsqrt(var + GN_EPS)
    rg = rg_ref[...]
    ret_y = (gn * gng_ref[...] + gnb_ref[...]) * (rg * jax.nn.sigmoid(rg))
    y = (jnp.dot(ret_y.astype(BF16), wout_ref[0:512, :], preferred_element_type=F32)
         + jnp.dot(mo_ref[...].astype(BF16), wout_ref[512:1024, :], preferred_element_type=F32))
    x1 = x_ref[...] + y
    x1_ref[...] = x1
    h2 = _rms_rows(x1, n2_ref[...]).astype(BF16)
    qm = jnp.dot(h2, wmq_ref[...], preferred_element_type=F32)
    qn = qn_ref[...]
    for h in range(H_MEM):
        sl = slice(h * HD_MEM, (h + 1) * HD_MEM)
        qm_ref[:, sl] = _rms_rows(qm[:, sl], qn[:, sl]) * (HD_MEM ** -0.5)


def _mix(x2d, ret_o, rg, moba_o, gn_g, gn_b, w_out_bf, norm2_g, w_mq_bf, qn512, ones64, tm):
    T = x2d.shape[0]
    row = lambda i: (i, 0)
    const = lambda i: (0, 0)
    return pl.pallas_call(
        _mix_kernel,
        grid=(T // tm,),
        in_specs=[pl.BlockSpec((tm, D_MODEL), row), pl.BlockSpec((tm, 512), row), pl.BlockSpec((tm, 512), row),
                  pl.BlockSpec((tm, 512), row), pl.BlockSpec((1, 512), const), pl.BlockSpec((1, 512), const),
                  pl.BlockSpec((D_MODEL, D_MODEL), const), pl.BlockSpec((1, D_MODEL), const),
                  pl.BlockSpec((D_MODEL, MEM_W), const), pl.BlockSpec((1, MEM_W), const),
                  pl.BlockSpec((512, 512), const)],
        out_specs=[pl.BlockSpec((tm, D_MODEL), row), pl.BlockSpec((tm, MEM_W), row)],
        out_shape=[jax.ShapeDtypeStruct((T, D_MODEL), F32), jax.ShapeDtypeStruct((T, MEM_W), F32)],
        compiler_params=_cparams(("parallel",), VMEM_LIMIT),
        name="mix_out",
    )(x2d, ret_o, rg, moba_o, gn_g.reshape(1, -1), gn_b.reshape(1, -1), w_out_bf, norm2_g.reshape(1, -1),
      w_mq_bf, qn512, ones64)


def _memkv_kernel(m_ref, g_ref, w_ref, kn_ref, k_ref, v_ref):
    mn = _rms_rows(m_ref[...], g_ref[...]).astype(BF16)
    k = jnp.dot(mn, w_ref[:, 0:MEM_W], preferred_element_type=F32)
    kn = kn_ref[...]
    for h in range(H_MEM):
        sl = slice(h * HD_MEM, (h + 1) * HD_MEM)
        k_ref[:, sl] = _rms_rows(k[:, sl], kn[:, sl])
    v_ref[...] = jnp.dot(mn, w_ref[:, MEM_W:2 * MEM_W], preferred_element_type=F32)


def _memkv(mem2d, mem_norm_g, w_mkv_bf, kn512, tm):
    T = mem2d.shape[0]
    row = lambda i: (i, 0)
    const = lambda i: (0, 0)
    out = jax.ShapeDtypeStruct((T, MEM_W), F32)
    return pl.pallas_call(
        _memkv_kernel,
        grid=(T // tm,),
        in_specs=[pl.BlockSpec((tm, D_MODEL), row), pl.BlockSpec((1, D_MODEL), const),
                  pl.BlockSpec((D_MODEL, 2 * MEM_W), const), pl.BlockSpec((1, MEM_W), const)],
        out_specs=[pl.BlockSpec((tm, MEM_W), row)] * 2,
        out_shape=[out, out],
        compiler_params=_cparams(("parallel",)),
        name="mem_kv",
    )(mem2d, mem_norm_g.reshape(1, -1), w_mkv_bf, kn512)


def _memattn_body(q, k, v):
    outs = []
    for h in range(H_MEM):
        sl = slice(h * HD_MEM, (h + 1) * HD_MEM)
        s = _nt(q[:, sl].astype(BF16), k[:, sl].astype(BF16))
        m = jnp.max(s, axis=1, keepdims=True)
        e = jnp.exp(s - m)
        pr = e * (1.0 / jnp.sum(e, axis=1, keepdims=True))
        outs.append(jnp.dot(pr.astype(BF16), v[:, sl].astype(BF16), preferred_element_type=F32))
    return outs


def _memattn_prompt_kernel(q_ref, k_ref, v_ref, o_ref):
    outs = _memattn_body(q_ref[...], k_ref[...], v_ref[...])
    for h in range(H_MEM):
        o_ref[:, h * HD_MEM:(h + 1) * HD_MEM] = outs[h]


def _memattn_prompt(qm, mem_k, mem_v, n_seq, seq, tq):
    T = qm.shape[0]
    nq = seq // tq
    return pl.pallas_call(
        _memattn_prompt_kernel,
        grid=(n_seq, nq),
        in_specs=[pl.BlockSpec((tq, MEM_W), lambda b, i: (b * nq + i, 0)),
                  pl.BlockSpec((N_MEM, MEM_W), lambda b, i: (b, 0)),
                  pl.BlockSpec((N_MEM, MEM_W), lambda b, i: (b, 0))],
        out_specs=pl.BlockSpec((tq, MEM_W), lambda b, i: (b * nq + i, 0)),
        out_shape=jax.ShapeDtypeStruct((T, MEM_W), F32),
        compiler_params=_cparams(("parallel", "arbitrary")),
        name="memattn_prompt",
    )(qm, mem_k, mem_v)


def _memattn_sample_kernel(q_ref, k_ref, v_ref, o_ref):
    outs = _memattn_body(q_ref[0], k_ref[0], v_ref[0])
    for h in range(H_MEM):
        o_ref[0, :, h * HD_MEM:(h + 1) * HD_MEM] = outs[h]


def _memattn_sample(qm, cache_k, cache_v, n_seq, t_dec):
    pad_rows = 16
    q3 = jnp.pad(qm.reshape(n_seq, t_dec, MEM_W), ((0, 0), (0, pad_rows - t_dec), (0, 0)))
    ck = cache_k.reshape(n_seq, N_MEM, MEM_W)
    cv = cache_v.reshape(n_seq, N_MEM, MEM_W)
    b3 = lambda b: (b, 0, 0)
    out = pl.pallas_call(
        _memattn_sample_kernel,
        grid=(n_seq,),
        in_specs=[pl.BlockSpec((1, pad_rows, MEM_W), b3), pl.BlockSpec((1, N_MEM, MEM_W), b3),
                  pl.BlockSpec((1, N_MEM, MEM_W), b3)],
        out_specs=pl.BlockSpec((1, pad_rows, MEM_W), b3),
        out_shape=jax.ShapeDtypeStruct((n_seq, pad_rows, MEM_W), F32),
        compiler_params=_cparams(("parallel",)),
        name="memattn_sample",
    )(q3, ck, cv)
    return out[:, :t_dec].reshape(n_seq * t_dec, MEM_W)


def _topk_rows(s, k, payload=None):
    n = s.shape[0]
    rowf = lax.broadcasted_iota(I32, s.shape, 0).astype(F32)
    vals, idxs, pays = [], [], []
    for _ in range(k):
        mx = jnp.max(s, axis=0, keepdims=True)
        ix = jnp.min(jnp.where(s == mx, rowf, float(n)), axis=0, keepdims=True)
        hit = rowf == ix
        vals.append(mx)
        idxs.append(ix)
        if payload is not None:
            pays.append(jnp.sum(jnp.where(hit, payload, 0.0), axis=0, keepdims=True))
        s = jnp.where(hit, -jnp.inf, s)
    cat = lambda xs: jnp.concatenate(xs, axis=0)
    return cat(vals), cat(idxs), (cat(pays) if payload is not None else None)


def _route_kernel(x1_ref, om_ref, wmo_ref, n3_ref, wq_ref, skh_ref, skl_ref,
                  x2_ref, h3_ref, e_ref, g_ref, q_scr, et_scr, gt_scr):
    x2 = x1_ref[...] + jnp.dot(om_ref[...].astype(BF16), wmo_ref[...], preferred_element_type=F32)
    x2_ref[...] = x2
    h3 = _rms_rows(x2, n3_ref[...])
    h3_ref[...] = h3
    h3b = h3.astype(BF16)
    for j in range(2 * PEER_HEADS):
        q_scr[j] = jnp.dot(h3b, wq_ref[:, j * LANES:(j + 1) * LANES], preferred_element_type=F32)

    K = PEER_TOPK

    def head_body(h, carry):
        sv, si = [], []
        for c in range(2):
            q = q_scr[2 * h + c]
            qh, ql = _split_hi_lo(q)
            kh = skh_ref[2 * h + c]
            s = _nt(kh, qh) + _nt(kh, ql) + _nt(skl_ref[2 * h + c], qh)
            v, ix, _ = _topk_rows(s, K)
            sv.append(v)
            si.append(ix)
        cand = [sv[0][0:1] + sv[1]]
        cidx = [si[0][0:1] * PEER_NKEYS + si[1]]
        for a in range(1, 8):
            cand.append(sv[0][a:a + 1] + sv[1][0:8])
            cidx.append(si[0][a:a + 1] * PEER_NKEYS + si[1][0:8])
        cand.append(sv[0][8:16] + sv[1][0:1])
        cidx.append(si[0][8:16] * PEER_NKEYS + si[1][0:1])
        best, _, eidx = _topk_rows(jnp.concatenate(cand, axis=0), K, jnp.concatenate(cidx, axis=0))
        e = jnp.exp(best - best[0:1])
        gate = e * (1.0 / jnp.sum(e, axis=0, keepdims=True))
        r0 = pl.multiple_of(h * K, K)
        et_scr[pl.ds(r0, K), :] = eidx
        gt_scr[pl.ds(r0, K), :] = gate
        return carry

    lax.fori_loop(0, PEER_HEADS, head_body, 0)
    e_ref[...] = et_scr[...].T.astype(I32)
    g_ref[...] = gt_scr[...].T


def _route(x1, omem, w_mo_bf, norm3_g, wq_bf, sk_hi, sk_lo, tm):
    T = x1.shape[0]
    row = lambda i: (i, 0)
    const = lambda i: (0, 0)
    c3 = lambda i: (0, 0, 0)
    return pl.pallas_call(
        _route_kernel,
        grid=(T // tm,),
        in_specs=[pl.BlockSpec((tm, D_MODEL), row), pl.BlockSpec((tm, MEM_W), row),
                  pl.BlockSpec((MEM_W, D_MODEL), const), pl.BlockSpec((1, D_MODEL), const),
                  pl.BlockSpec((D_MODEL, PEER_HEADS * PEER_DQ), const),
                  pl.BlockSpec((2 * PEER_HEADS, PEER_NKEYS, LANES), c3),
                  pl.BlockSpec((2 * PEER_HEADS, PEER_NKEYS, LANES), c3)],
        out_specs=[pl.BlockSpec((tm, D_MODEL), row), pl.BlockSpec((tm, D_MODEL), row),
                   pl.BlockSpec((tm, PEER_E), row), pl.BlockSpec((tm, PEER_E), row)],
        out_shape=[jax.ShapeDtypeStruct((T, D_MODEL), F32), jax.ShapeDtypeStruct((T, D_MODEL), F32),
                   jax.ShapeDtypeStruct((T, PEER_E), I32), jax.ShapeDtypeStruct((T, PEER_E), F32)],
        scratch_shapes=[pltpu.VMEM((2 * PEER_HEADS, tm, LANES), F32), pltpu.VMEM((PEER_E, tm), F32),
                        pltpu.VMEM((PEER_E, tm), F32)],
        compiler_params=_cparams(("parallel",), VMEM_LIMIT),
        name="peer_route",
    )(x1, omem, w_mo_bf, norm3_g.reshape(1, -1), wq_bf, sk_hi, sk_lo)


PEER_CH = 64
PEER_SC_TB = 16
PEER_WORDS = D_MODEL // 2


def _sc_mesh():
    return plsc.VectorSubcoreMesh(core_axis_name="c", subcore_axis_name="s")


def _pack_bf16_pairs(table):
    n, d = table.shape
    tb16 = table.astype(BF16)
    pairs = jnp.stack([tb16[:, :d // 2], tb16[:, d // 2:]], axis=-1)
    return lax.bitcast_convert_type(pairs, I32)


def _unpack_words(wd):
    lo = lax.bitcast_convert_type(lax.shift_left(wd, jnp.full(wd.shape, 16, I32)), F32)
    hi = lax.bitcast_convert_type(wd & jnp.full(wd.shape, -65536, I32), F32)
    return lo, hi


def _peer_dot_sc(h3, u_words, eidx, tb):
    T = h3.shape[0]
    tpw = T // SC_WORKERS
    nch = PEER_E // PEER_CH
    idx2 = eidx.reshape(T * nch, PEER_CH)

    @functools.partial(
        pl.kernel, mesh=_sc_mesh(),
        out_type=jax.ShapeDtypeStruct((T * PEER_E,), F32),
        scratch_types=[
            pltpu.VMEM((tb * nch, PEER_CH), I32),
            pltpu.VMEM((tb, D_MODEL), F32),
            pltpu.VMEM((PEER_CH, PEER_WORDS), I32),
            pltpu.VMEM((PEER_CH, PEER_WORDS), I32),
            pltpu.VMEM((tb * PEER_E,), F32),
            pltpu.VMEM((SC_LANES * SC_LANES,), F32),
            pltpu.SemaphoreType.DMA,
            pltpu.SemaphoreType.DMA,
        ],
        compiler_params=pltpu.CompilerParams(needs_layout_passes=False),
        name="peer_dot_sc",
    )
    def k(h_hbm, u_hbm, idx_hbm, out_hbm, idx_v, h_v, rows0, rows1, a_v, accm, sem0, sem1):
        wid = lax.axis_index("s") * SC_CORES + lax.axis_index("c")
        tok0 = wid * tpw
        lanes = lax.iota(I32, SC_LANES)

        def gather(g, rows, sem):
            return pltpu.make_async_copy(u_hbm.at[idx_v.at[g]], rows, sem)

        def compute(g, rows):
            tl = g // nch
            for grp in range(PEER_CH // SC_LANES):
                def body(c, accs):
                    h_lo = h_v[tl, pl.ds(c * SC_LANES, SC_LANES)]
                    h_hi = h_v[tl, pl.ds(PEER_WORDS + c * SC_LANES, SC_LANES)]
                    out = []
                    for r in range(SC_LANES):
                        lo, hi = _unpack_words(rows[grp * SC_LANES + r, pl.ds(c * SC_LANES, SC_LANES)])
                        out.append(accs[r] + lo * h_lo + hi * h_hi)
                    return tuple(out)
                accs = lax.fori_loop(0, PEER_WORDS // SC_LANES, body,
                                     tuple(jnp.zeros((SC_LANES,), F32) for _ in range(SC_LANES)))
                for r in range(SC_LANES):
                    accm[pl.ds(r * SC_LANES, SC_LANES)] = accs[r]
                tot = jnp.zeros((SC_LANES,), F32)
                for l in range(SC_LANES):
                    tot = tot + plsc.load_gather(accm, [lanes * SC_LANES + l])
                a_v[pl.ds(g * PEER_CH + grp * SC_LANES, SC_LANES)] = tot

        @pl.loop(0, tpw // tb)
        def _(blk):
            t0 = tok0 + blk * tb
            pltpu.sync_copy(idx_hbm.at[pl.ds(t0 * nch, tb * nch)], idx_v)
            pltpu.sync_copy(h_hbm.at[pl.ds(t0, tb)], h_v)
            gather(0, rows0, sem0).start()

            @pl.loop(0, tb * nch, step=2)
            def _(g):
                gather(g + 1, rows1, sem1).start()
                gather(g, rows0, sem0).wait()
                compute(g, rows0)

                @pl.when(g + 2 < tb * nch)
                def _():
                    gather(g + 2, rows0, sem0).start()

                gather(g + 1, rows1, sem1).wait()
                compute(g + 1, rows1)

            pltpu.sync_copy(a_v, out_hbm.at[pl.ds(t0 * PEER_E, tb * PEER_E)])

    return k(h3, u_words, idx2).reshape(T, PEER_E)


def _peer_sum_sc(x2, w, v_words, eidx, tb):
    T = x2.shape[0]
    tpw = T // SC_WORKERS
    nch = PEER_E // PEER_CH
    idx2 = eidx.reshape(T * nch, PEER_CH)
    CG = 16
    ncg = PEER_WORDS // (CG * SC_LANES)

    @functools.partial(
        pl.kernel, mesh=_sc_mesh(),
        out_type=jax.ShapeDtypeStruct((T, D_MODEL), F32),
        scratch_types=[
            pltpu.VMEM((tb * nch, PEER_CH), I32),
            pltpu.VMEM((tb * PEER_E,), F32),
            pltpu.VMEM((PEER_CH, PEER_WORDS), I32),
            pltpu.VMEM((PEER_CH, PEER_WORDS), I32),
            pltpu.VMEM((tb, D_MODEL), F32),
            pltpu.SemaphoreType.DMA,
            pltpu.SemaphoreType.DMA,
        ],
        compiler_params=pltpu.CompilerParams(needs_layout_passes=False),
        name="peer_sum_sc",
    )
    def k(x_hbm, w_hbm, v_hbm, idx_hbm, out_hbm, idx_v, w_v, rows0, rows1, y_v, sem0, sem1):
        wid = lax.axis_index("s") * SC_CORES + lax.axis_index("c")
        tok0 = wid * tpw
        zero_i = jnp.zeros((SC_LANES,), I32)

        def gather(g, rows, sem):
            return pltpu.make_async_copy(v_hbm.at[idx_v.at[g]], rows, sem)

        def compute(g, rows):
            tl = g // nch
            for cg in range(ncg):
                base = cg * CG * SC_LANES

                def body(r, accs):
                    wb = plsc.load_gather(w_v, [zero_i + (g * PEER_CH + r)])
                    out_lo, out_hi = [], []
                    for j in range(CG):
                        lo, hi = _unpack_words(rows[r, pl.ds(base + j * SC_LANES, SC_LANES)])
                        out_lo.append(accs[j] + lo * wb)
                        out_hi.append(accs[CG + j] + hi * wb)
                    return tuple(out_lo + out_hi)
                init = tuple([y_v[tl, pl.ds(base + j * SC_LANES, SC_LANES)] for j in range(CG)]
                             + [y_v[tl, pl.ds(PEER_WORDS + base + j * SC_LANES, SC_LANES)] for j in range(CG)])
                accs = lax.fori_loop(0, PEER_CH, body, init)
                for j in range(CG):
                    y_v[tl, pl.ds(base + j * SC_LANES, SC_LANES)] = accs[j]
                    y_v[tl, pl.ds(PEER_WORDS + base + j * SC_LANES, SC_LANES)] = accs[CG + j]

        @pl.loop(0, tpw // tb)
        def _(blk):
            t0 = tok0 + blk * tb
            pltpu.sync_copy(idx_hbm.at[pl.ds(t0 * nch, tb * nch)], idx_v)
            pltpu.sync_copy(w_hbm.at[pl.ds(t0 * PEER_E, tb * PEER_E)], w_v)
            pltpu.sync_copy(x_hbm.at[pl.ds(t0, tb)], y_v)
            gather(0, rows0, sem0).start()

            @pl.loop(0, tb * nch, step=2)
            def _(g):
                gather(g + 1, rows1, sem1).start()
                gather(g, rows0, sem0).wait()
                compute(g, rows0)

                @pl.when(g + 2 < tb * nch)
                def _():
                    gather(g + 2, rows0, sem0).start()

                gather(g + 1, rows1, sem1).wait()
                compute(g + 1, rows1)

            pltpu.sync_copy(y_v, out_hbm.at[pl.ds(t0, tb)])

    return k(x2, w.reshape(-1), v_words, idx2)


def _weight_kernel(g_ref, a_ref, w_ref):
    a = a_ref[...]
    w_ref[...] = g_ref[...] * (0.5 * a * (1.0 + lax.erf(a * (1.0 / math.sqrt(2.0)))))


def _peer_weight(g, a, tm):
    T = g.shape[0]
    row = lambda i: (i, 0)
    return pl.pallas_call(
        _weight_kernel,
        grid=(T // tm,),
        in_specs=[pl.BlockSpec((tm, PEER_E), row)] * 2,
        out_specs=pl.BlockSpec((tm, PEER_E), row),
        out_shape=jax.ShapeDtypeStruct((T, PEER_E), F32),
        compiler_params=_cparams(("parallel",)),
        name="peer_weight",
    )(g, a)


def _tail(x2d, ret_o, rg, moba_o, memattn, weights, tm, sc_tb):
    (gn_g, gn_b, w_out_bf, norm2_g, w_mq_bf, mem_qn512, ones64, w_mo_bf, norm3_g, wq_bf, sk_hi, sk_lo, u, v) = weights
    x1, qm = _mix(x2d, ret_o, rg, moba_o, gn_g, gn_b, w_out_bf, norm2_g, w_mq_bf, mem_qn512, ones64, tm)
    om = memattn(qm)
    x2, h3, eidx, gate = _route(x1, om, w_mo_bf, norm3_g, wq_bf, sk_hi, sk_lo, tm)
    a = _peer_dot_sc(h3, u, eidx, sc_tb)
    w = _peer_weight(gate, a, tm)
    return _peer_sum_sc(x2, w, v, eidx, sc_tb)


def kernel(x_prompt, x_sample, state_ret, cache_moba_k, cache_moba_v, page_table, cache_mem_k, cache_mem_v, mem_prompt, norm1_g, w_in, ret_gn_g, ret_gn_b, moba_qn_g, moba_kn_g, w_out, norm2_g, mem_norm_g, w_mq, w_mkv, mem_qn_g, mem_kn_g, w_mo, norm3_g, peer_wq, peer_sub_keys, peer_u, peer_v):
    n_b, seq, _ = x_prompt.shape
    n_d, t_dec, _ = x_sample.shape
    past_len = page_table.shape[1] * PAGE_SIZE
    tm = 256

    w_in_bf = w_in.astype(BF16)
    w_out_bf = w_out.astype(BF16)
    w_mq_bf = w_mq.astype(BF16)
    w_mkv_bf = w_mkv.astype(BF16)
    w_mo_bf = w_mo.astype(BF16)
    wq_bf = peer_wq.astype(BF16)
    sk = peer_sub_keys.reshape(2 * PEER_HEADS, PEER_NKEYS, PEER_DQ // 2)
    sk_hi = sk.astype(BF16)
    sk_lo = (sk - sk_hi.astype(F32)).astype(BF16)
    ones64 = _block_ones(512, 64)
    tile8 = lambda g: jnp.tile(g, H_MOBA).reshape(1, 512)
    tile4 = lambda g: jnp.tile(g, H_MEM).reshape(1, 512)
    tail_w = (ret_gn_g, ret_gn_b, w_out_bf, norm2_g, w_mq_bf, tile4(mem_qn_g), ones64, w_mo_bf, norm3_g, wq_bf,
              sk_hi, sk_lo, _pack_bf16_pairs(peer_u), _pack_bf16_pairs(peer_v))

    n_c = n_b // PROMPT_CHUNKS
    parts = []
    for c in range(PROMPT_CHUNKS):
        xp = x_prompt[c * n_c:(c + 1) * n_c].reshape(n_c * seq, D_MODEL)
        rq, rk, rv, rg, mq, mk, mv = _proj(xp, norm1_g, w_in_bf, tile8(moba_qn_g), tile8(moba_kn_g), ones64, tm)
        ret_o, st_p = _retention(rq, rk, rv, jnp.zeros((n_c, 4, LANES, LANES), F32), RET_CHUNK, RET_CHUNK,
                                 n_c, seq // RET_CHUNK)
        moba_o = _moba_prompt(mq, mk, mv, n_c, seq)
        mem_k, mem_v = _memkv(mem_prompt[c * n_c:(c + 1) * n_c].reshape(n_c * N_MEM, D_MODEL), mem_norm_g,
                              w_mkv_bf, tile4(mem_kn_g), tm)
        y_p = _tail(xp, ret_o, rg, moba_o,
                    functools.partial(_memattn_prompt, mem_k=mem_k, mem_v=mem_v, n_seq=n_c, seq=seq, tq=512),
                    tail_w, tm, PEER_SC_TB)
        parts.append((y_p, st_p, mk, mv, mem_k, mem_v))
    y_p, st_p, mk, mv, mem_k, mem_v = (jnp.concatenate(p, axis=0) for p in zip(*parts))

    xs = x_sample.reshape(n_d * t_dec, D_MODEL)
    srq, srk, srv, srg, smq, smk, smv = _proj(xs, norm1_g, w_in_bf, tile8(moba_qn_g), tile8(moba_kn_g), ones64, tm)
    rows = 16
    pad = lambda a: jnp.pad(a.reshape(n_d, t_dec, 512), ((0, 0), (0, rows - t_dec), (0, 0))).reshape(n_d * rows, 512)
    sret_o, st_s = _retention(pad(srq), pad(srk), pad(srv), _to_block_diag(state_ret), t_dec, rows, n_d, 1)
    sret_o = sret_o.reshape(n_d, rows, 512)[:, :t_dec].reshape(n_d * t_dec, 512)
    smoba_o = _moba_sample(smq, smk, smv, cache_moba_k, cache_moba_v, page_table, past_len)
    y_s = _tail(xs, sret_o, srg, smoba_o,
                lambda qm: _memattn_sample(qm, cache_mem_k, cache_mem_v, n_d, t_dec), tail_w, tm, PEER_SC_TB)

    return (y_p.reshape(n_b, seq, D_MODEL), y_s.reshape(n_d, t_dec, D_MODEL),
            _from_block_diag(st_p),
            mk.reshape(n_b, seq, H_MOBA, HD_MOBA), mv.reshape(n_b, seq, H_MOBA, HD_MOBA),
            mem_k.reshape(n_b, N_MEM, H_MEM, HD_MEM), mem_v.reshape(n_b, N_MEM, H_MEM, HD_MEM),
            _from_block_diag(st_s),
            smk.reshape(n_d, t_dec, H_MOBA, HD_MOBA), smv.reshape(n_d, t_dec, H_MOBA, HD_MOBA))
```

```python
import functools
import math

import jax
import jax.numpy as jnp
from jax import lax
from jax.experimental import pallas as pl
from jax.experimental.pallas import tpu as pltpu
from jax.experimental.pallas import tpu_sc as plsc

F32 = jnp.float32
BF16 = jnp.bfloat16
I32 = jnp.int32

D_MODEL = 1024
H_RET = 8
DK_RET = 64
RET_CHUNK = 128
RET_W = 512
H_MOBA = 8
HD_MOBA = 64
MOBA_BLOCK = 256
MOBA_TOPK = 3
MOBA_W = 512
IN_W = 3584
N_MEM = 256
H_MEM = 4
HD_MEM = 128
MEM_W = 512
PAGE_SIZE = 128
PEER_HEADS = 8
PEER_NKEYS = 128
PEER_DQ = 256
PEER_TOPK = 16
PEER_E = PEER_HEADS * PEER_TOPK
RMS_EPS = 1e-6
GN_EPS = 1e-5

LANES = 128
SC_CORES = 2
SC_SUBCORES = 16
SC_LANES = 16
SC_WORKERS = SC_CORES * SC_SUBCORES
VMEM_LIMIT = 56 * 1024 * 1024

MOBA_SAMPLE_PPS = 8
SAMPLE_MOBA_PARTS = 4
PROMPT_CHUNKS = 8

NEG = -1e30
NT_DIMS = (((1,), (1,)), ((), ()))


def _split_hi_lo(x):
    hi = x.astype(BF16)
    lo = (x - hi.astype(F32)).astype(BF16)
    return hi, lo


def _nt(a, b):
    return lax.dot_general(a, b, NT_DIMS, preferred_element_type=F32)


def _nt3(a, b):
    ah, al = _split_hi_lo(a)
    bh, bl = _split_hi_lo(b)
    return _nt(ah, bh) + _nt(ah, bl) + _nt(al, bh)


def _group_sum(x, ones_bd):
    hi, lo = _split_hi_lo(x)
    return (jnp.dot(hi, ones_bd, preferred_element_type=F32)
            + jnp.dot(lo, ones_bd, preferred_element_type=F32))


def _rms_rows(x, g):
    return x * lax.rsqrt(jnp.mean(x * x, axis=-1, keepdims=True) + RMS_EPS) * g


def _block_ones(width, group):
    i = jnp.arange(width) // group
    return (i[:, None] == i[None, :]).astype(BF16)


def _cparams(sem, vmem=None):
    return pltpu.CompilerParams(dimension_semantics=sem, vmem_limit_bytes=vmem)


def _proj_kernel(x_ref, g1_ref, w_ref, qn_ref, kn_ref, ones_ref, after_ref,
                 rq_ref, rk_ref, rv_ref, rg_ref, mq_ref, mk_ref, mv_ref):
    xn = _rms_rows(x_ref[...], g1_ref[...]).astype(BF16)

    def col(i):
        return jnp.dot(xn, w_ref[:, i * 512:(i + 1) * 512], preferred_element_type=F32)

    rq_ref[...] = col(0)
    rk_ref[...] = col(1) * (DK_RET ** -0.5)
    rv_ref[...] = col(2)
    rg_ref[...] = col(3)
    ones = ones_ref[...]

    def headnorm(a, g):
        ms = _group_sum(a * a, ones) * (1.0 / HD_MOBA)
        return a * lax.rsqrt(ms + RMS_EPS) * g

    mq_ref[...] = headnorm(col(4), qn_ref[...]) * (HD_MOBA ** -0.5)
    mk_ref[...] = headnorm(col(5), kn_ref[...])
    mv_ref[...] = col(6)


def _proj(x2d, norm1_g, w_in_bf, qn512, kn512, ones64, tm, after):
    T = x2d.shape[0]
    out = jax.ShapeDtypeStruct((T, 512), F32)
    row = lambda i: (i, 0)
    const = lambda i: (0, 0)
    return pl.pallas_call(
        _proj_kernel,
        grid=(T // tm,),
        in_specs=[pl.BlockSpec((tm, D_MODEL), row), pl.BlockSpec((1, D_MODEL), const),
                  pl.BlockSpec((D_MODEL, IN_W), const), pl.BlockSpec((1, 512), const),
                  pl.BlockSpec((1, 512), const), pl.BlockSpec((512, 512), const),
                  pl.BlockSpec(memory_space=pl.ANY)],
        out_specs=[pl.BlockSpec((tm, 512), row)] * 7,
        out_shape=[out] * 7,
        compiler_params=_cparams(("parallel",), VMEM_LIMIT),
        name="proj",
    )(x2d, norm1_g.reshape(1, -1), w_in_bf, qn512, kn512, ones64, after)


def _ret_kernel(q_ref, k_ref, v_ref, s0_ref, dec_ref, rowdec_ref, kdec_ref, sdec_ref, o_ref, st_ref, state_scr):
    c = pl.program_id(1)
    R = q_ref.shape[0]

    @pl.when(c == 0)
    def _():
        state_scr[...] = s0_ref[0]

    lane = lax.broadcasted_iota(I32, (R, LANES), 1)
    lo_lane = lane < 64
    bd = ((lax.broadcasted_iota(I32, (LANES, LANES), 0) < 64)
          == (lax.broadcasted_iota(I32, (LANES, LANES), 1) < 64))
    for p in range(4):
        sl = slice(p * LANES, (p + 1) * LANES)
        q2 = q_ref[:, sl]
        k2 = k_ref[:, sl]
        v2b = v_ref[:, sl].astype(BF16)
        k2b = k2.astype(BF16)
        qa = jnp.where(lo_lane, q2, 0.0).astype(BF16)
        qb = jnp.where(lo_lane, 0.0, q2).astype(BF16)
        sa = _nt(qa, k2b) * dec_ref[2 * p]
        sb = _nt(qb, k2b) * dec_ref[2 * p + 1]
        ia = jnp.dot(sa.astype(BF16), v2b, preferred_element_type=F32)
        ib = jnp.dot(sb.astype(BF16), v2b, preferred_element_type=F32)
        st = state_scr[p]
        cross = jnp.dot(q2.astype(BF16), st.astype(BF16), preferred_element_type=F32) * rowdec_ref[p]
        o_ref[:, sl] = jnp.where(lo_lane, ia, ib) + cross
        kdt = (k2 * kdec_ref[p]).T.astype(BF16)
        upd = jnp.dot(kdt, v2b, preferred_element_type=F32)
        state_scr[p] = jnp.where(bd, st * sdec_ref[p] + upd, 0.0)

    @pl.when(c == pl.num_programs(1) - 1)
    def _():
        st_ref[0] = state_scr[...]


def _ret_tables(t_real, rows):
    h = jnp.arange(H_RET, dtype=F32)
    log_g = jnp.log1p(-jnp.exp2(-5.0 - h))
    pos = jnp.arange(rows, dtype=F32)
    diff = pos[:, None] - pos[None, :]
    ok = (diff >= 0) & (pos[:, None] < t_real) & (pos[None, :] < t_real)
    dec = jnp.where(ok[None], jnp.exp(log_g[:, None, None] * jnp.where(ok, diff, 0.0)[None]), 0.0)
    lg_lane = jnp.repeat(log_g, 64).reshape(4, 1, LANES)
    rowdec = jnp.exp(lg_lane * (pos[None, :, None] + 1.0))
    kdec = jnp.exp(lg_lane * jnp.maximum(t_real - 1.0 - pos, 0.0)[None, :, None])
    sdec = jnp.exp(lg_lane * float(t_real))
    return dec, rowdec, kdec, sdec


def _to_block_diag(state):
    n = state.shape[0]
    s = state.reshape(n, 4, 2, 64, 64)
    z = jnp.zeros_like(s[:, :, 0])
    top = jnp.concatenate([s[:, :, 0], z], axis=-1)
    bot = jnp.concatenate([z, s[:, :, 1]], axis=-1)
    return jnp.concatenate([top, bot], axis=-2)


def _from_block_diag(sbd):
    n = sbd.shape[0]
    a = sbd[:, :, :64, :64]
    b = sbd[:, :, 64:, 64:]
    return jnp.stack([a, b], axis=2).reshape(n, H_RET, 64, 64)


def _retention(q, k, v, state_bd, t_real, rows, n_seq, n_chunks):
    dec, rowdec, kdec, sdec = _ret_tables(t_real, rows)
    T = q.shape[0]
    tok = lambda s, c: (s * n_chunks + c, 0)
    c3 = lambda s, c: (0, 0, 0)
    st = lambda s, c: (s, 0, 0, 0)
    return pl.pallas_call(
        _ret_kernel,
        grid=(n_seq, n_chunks),
        in_specs=[pl.BlockSpec((rows, 512), tok)] * 3 + [
            pl.BlockSpec((1, 4, LANES, LANES), st),
            pl.BlockSpec((H_RET, rows, rows), c3), pl.BlockSpec((4, rows, LANES), c3),
            pl.BlockSpec((4, rows, LANES), c3), pl.BlockSpec((4, 1, LANES), c3)],
        out_specs=[pl.BlockSpec((rows, 512), tok), pl.BlockSpec((1, 4, LANES, LANES), st)],
        out_shape=[jax.ShapeDtypeStruct((T, 512), F32), jax.ShapeDtypeStruct((n_seq, 4, LANES, LANES), F32)],
        scratch_shapes=[pltpu.VMEM((4, LANES, LANES), F32)],
        compiler_params=_cparams(("parallel", "arbitrary")),
        name="retention",
    )(q, k, v, state_bd, dec, rowdec, kdec, sdec)


def _moba_prompt_kernel(q_ref, k_ref, v_ref, alibi_ref, slope_ref, o_ref,
                        kb_scr, vt_scr, km_scr, bias_scr, m_scr, l_scr, acc_scr):
    i = pl.program_id(2)
    nb = km_scr.shape[0]
    B = MOBA_BLOCK

    @pl.when(i == 0)
    def _():
        for j in range(nb):
            kj = k_ref[j * B:(j + 1) * B, :]
            kb_scr[j * B:(j + 1) * B, :] = kj.astype(BF16)
            km_scr[j:j + 1, :] = jnp.sum(kj, axis=0, keepdims=True) * (1.0 / B)
            vt_scr[:, j * B:(j + 1) * B] = v_ref[j * B:(j + 1) * B, :].T.astype(BF16)

    q2 = q_ref[...]
    lane = lax.broadcasted_iota(I32, (B, LANES), 1)
    qq = jnp.concatenate([jnp.where(lane < 64, q2, 0.0), jnp.where(lane < 64, 0.0, q2)], axis=0)
    qqb = qq.astype(BF16)

    blk = lax.broadcasted_iota(I32, (nb, 2 * B), 0)
    past = blk < i
    g = jnp.where(past, _nt3(km_scr[...], qq), -jnp.inf)
    rank = jnp.zeros((nb, 2 * B), I32)
    for jj in range(nb):
        gj = g[jj:jj + 1, :]
        rank = rank + ((gj > g) | ((gj == g) & (jj < blk))).astype(I32)
    bias_scr[...] = jnp.where(past & (rank < MOBA_TOPK), 0.0, NEG)

    alibi = alibi_ref[0]
    slope_b = slope_ref[0]

    key_r = lax.broadcasted_iota(I32, (B, 2 * B), 0)
    qry_c = lax.broadcasted_iota(I32, (B, 2 * B), 1)
    qry_c = jnp.where(qry_c >= B, qry_c - B, qry_c)
    own = pl.multiple_of(i * B, B)
    s = jnp.where(key_r <= qry_c, _nt(kb_scr[pl.ds(own, B), :], qqb) - alibi, NEG)
    m0 = jnp.max(s, axis=0, keepdims=True)
    p0 = jnp.exp(s - m0)
    m_scr[...] = m0
    l_scr[...] = jnp.sum(p0, axis=0, keepdims=True)
    acc_scr[...] = jnp.dot(vt_scr[:, pl.ds(own, B)], p0.astype(BF16), preferred_element_type=F32)

    def body(j, carry):
        off = pl.multiple_of(j * B, B)
        rowb = bias_scr[pl.ds(j, 1), :] - slope_b * (i - j).astype(F32)
        sj = _nt(kb_scr[pl.ds(off, B), :], qqb) - alibi + rowb
        m_old = m_scr[...]
        m_new = jnp.maximum(m_old, jnp.max(sj, axis=0, keepdims=True))
        alpha = jnp.exp(m_old - m_new)
        pj = jnp.exp(sj - m_new)
        l_scr[...] = alpha * l_scr[...] + jnp.sum(pj, axis=0, keepdims=True)
        acc_scr[...] = alpha * acc_scr[...] + jnp.dot(vt_scr[:, pl.ds(off, B)], pj.astype(BF16),
                                                      preferred_element_type=F32)
        m_scr[...] = m_new
        return carry

    lax.fori_loop(0, i, body, 0)
    res = acc_scr[...] * (1.0 / l_scr[...])
    row = lax.broadcasted_iota(I32, (LANES, B), 0)
    o_ref[...] = jnp.where(row < 64, res[:, :B], res[:, B:]).T


def _moba_tables():
    slopes = jnp.exp2(-8.0 * (jnp.arange(H_MOBA, dtype=F32) + 1.0) / H_MOBA)
    r = jnp.arange(MOBA_BLOCK, dtype=F32)
    rel = r[None, :] - r[:, None]
    alibi = (slopes[:, None, None] * rel[None]).reshape(4, 2, MOBA_BLOCK, MOBA_BLOCK)
    alibi = jnp.concatenate([alibi[:, 0], alibi[:, 1]], axis=-1)
    slope_b = jnp.repeat((slopes * MOBA_BLOCK).reshape(4, 1, 2), MOBA_BLOCK, axis=-1)
    return slopes, alibi, slope_b


def _moba_prompt(mq, mk, mv, n_seq, seq):
    _, alibi, slope_b = _moba_tables()
    nq = seq // MOBA_BLOCK
    T = mq.shape[0]
    B = MOBA_BLOCK
    return pl.pallas_call(
        _moba_prompt_kernel,
        grid=(n_seq, 4, nq),
        in_specs=[pl.BlockSpec((B, LANES), lambda b, p, i: (b * nq + i, p)),
                  pl.BlockSpec((seq, LANES), lambda b, p, i: (b, p)),
                  pl.BlockSpec((seq, LANES), lambda b, p, i: (b, p)),
                  pl.BlockSpec((1, B, 2 * B), lambda b, p, i: (p, 0, 0)),
                  pl.BlockSpec((1, 1, 2 * B), lambda b, p, i: (p, 0, 0))],
        out_specs=pl.BlockSpec((B, LANES), lambda b, p, i: (b * nq + i, p)),
        out_shape=jax.ShapeDtypeStruct((T, 512), F32),
        scratch_shapes=[pltpu.VMEM((seq, LANES), BF16), pltpu.VMEM((LANES, seq), BF16),
                        pltpu.VMEM((nq, LANES), F32), pltpu.VMEM((nq, 2 * B), F32),
                        pltpu.VMEM((1, 2 * B), F32), pltpu.VMEM((1, 2 * B), F32),
                        pltpu.VMEM((LANES, 2 * B), F32)],
        compiler_params=_cparams(("parallel", "parallel", "arbitrary")),
        name="moba_prompt",
    )(mq, mk, mv, alibi, slope_b)


def _moba_sample_kernel(pt_ref, q_ref, kn_ref, vn_ref, *rest, past_len, pps):
    k_refs, v_refs = rest[:pps], rest[pps:2 * pps]
    (alibi_ref, slopec_ref, hm_ref, trow_ref, after_ref, o_ref,
     qbd_scr, knew_scr, vnew_scr, m_scr, l_scr, g_scr, o_scr) = rest[2 * pps:]
    step = pl.program_id(1)
    R = qbd_scr.shape[0]
    lanei = lax.broadcasted_iota(I32, (R, LANES), 1)

    @pl.when(step == 0)
    def _():
        q = q_ref[0]
        hm = hm_ref[...]
        for t in range(4):
            qbd_scr[t * 8:(t + 1) * 8, :] = jnp.broadcast_to(q[t:t + 1, :], (8, 512)) * hm
        knew_scr[...] = jnp.zeros_like(knew_scr)
        vnew_scr[...] = jnp.zeros_like(vnew_scr)
        knew_scr[0:4, :] = kn_ref[0]
        vnew_scr[0:4, :] = vn_ref[0]
        m_scr[...] = jnp.zeros_like(m_scr)
        l_scr[...] = jnp.zeros_like(l_scr)
        g_scr[...] = jnp.zeros_like(g_scr)

    qbd_b = qbd_scr[...].astype(BF16)
    alibi = alibi_ref[...]
    slopec = slopec_ref[...]
    m_all = m_scr[...]
    l_all = l_scr[...]
    g_all = g_scr[...]
    raws = [jnp.dot(qbd_b, k_refs[j][0].astype(BF16), preferred_element_type=F32) for j in range(pps)]
    es = []
    for j in range(pps):
        p = step * pps + j
        off = (past_len - p * PAGE_SIZE).astype(F32)
        s = raws[j] - alibi - slopec * off
        m = jnp.max(s, axis=1, keepdims=True)
        e = jnp.exp(s - m)
        here = lanei == p
        g_all = jnp.where(here, jnp.sum(raws[j], axis=1, keepdims=True), g_all)
        m_all = jnp.where(here, m, m_all)
        l_all = jnp.where(here, jnp.sum(e, axis=1, keepdims=True), l_all)
        es.append(e.astype(BF16))
    for j in range(pps):
        o_scr[step * pps + j] = _nt(es[j], v_refs[j][0].astype(BF16))
    m_scr[...] = m_all
    l_scr[...] = l_all
    g_scr[...] = g_all

    @pl.when(step == pl.num_programs(1) - 1)
    def _():
        n_pg = past_len // PAGE_SIZE
        g = (g_all + pltpu.roll(g_all, LANES - 1, 1)) * (1.0 / MOBA_BLOCK)
        g = jnp.where((lanei < n_pg) & ((lanei & 1) == 0), g, -jnp.inf)
        sel = lanei < 0
        for _ in range(MOBA_TOPK):
            mx = jnp.max(g, axis=1, keepdims=True)
            ix = jnp.min(jnp.where(g == mx, lanei, LANES), axis=1, keepdims=True)
            pick = (lanei == ix) | (lanei == ix + 1)
            sel = sel | pick
            g = jnp.where(pick, -jnp.inf, g)
        sn = _nt(qbd_b, knew_scr[...].astype(BF16)) - alibi_ref[...]
        sn = jnp.where(lanei <= trow_ref[...], sn, NEG)
        mn = jnp.max(sn, axis=1, keepdims=True)
        en = jnp.exp(sn - mn)
        ln = jnp.sum(en, axis=1, keepdims=True)
        on = jnp.dot(en.astype(BF16), vnew_scr[...].astype(BF16), preferred_element_type=F32)
        m_all = m_scr[...]
        mt = jnp.maximum(jnp.max(jnp.where(sel, m_all, -jnp.inf), axis=1, keepdims=True), mn)
        w_all = jnp.where(sel, jnp.exp(m_all - mt), 0.0)
        wn = jnp.exp(mn - mt)
        den = jnp.sum(w_all * l_scr[...], axis=1, keepdims=True) + wn * ln
        num = wn * on
        for pg in range(n_pg):
            num = num + w_all[:, pg:pg + 1] * o_scr[pg]
        of = num * (1.0 / den)
        o_ref[0] = jnp.sum(of.reshape(4, 8, 512) * hm_ref[...][None], axis=1)


def _moba_sample_part(mq, mk, mv, cache_k, cache_v, page_table, past_len, after):
    n_seq, n_pages = page_table.shape
    slopes = jnp.exp2(-8.0 * (jnp.arange(H_MOBA, dtype=F32) + 1.0) / H_MOBA)
    row = jnp.arange(32)
    t_row = (row // 8).astype(F32)
    s_row = slopes[row % 8]
    lane = jnp.arange(LANES, dtype=F32)
    alibi = s_row[:, None] * (t_row[:, None] - lane[None, :])
    slopec = jnp.broadcast_to(s_row[:, None], (32, LANES))
    hm = (jnp.arange(512)[None, :] // 64 == jnp.arange(8)[:, None]).astype(F32)
    trow = jnp.broadcast_to((row // 8).astype(I32)[:, None], (32, LANES))
    n_pool = cache_k.shape[0]
    ck = jnp.transpose(cache_k, (0, 2, 3, 1)).reshape(n_pool, 512, PAGE_SIZE)
    cv = jnp.transpose(cache_v, (0, 2, 3, 1)).reshape(n_pool, 512, PAGE_SIZE)
    pps = MOBA_SAMPLE_PPS
    seq3 = lambda b, s, pt: (b, 0, 0)
    c2 = lambda b, s, pt: (0, 0)
    page_specs = [pl.BlockSpec((1, 512, PAGE_SIZE), lambda b, s, pt, j=j: (pt[b * n_pages + s * pps + j], 0, 0))
                  for j in range(pps)]
    grid_spec = pltpu.PrefetchScalarGridSpec(
        num_scalar_prefetch=1,
        grid=(n_seq, n_pages // pps),
        in_specs=[pl.BlockSpec((1, 4, 512), seq3)] * 3 + page_specs + page_specs + [
            pl.BlockSpec((32, LANES), c2), pl.BlockSpec((32, LANES), c2), pl.BlockSpec((8, 512), c2),
            pl.BlockSpec((32, LANES), c2), pl.BlockSpec(memory_space=pl.ANY)],
        out_specs=pl.BlockSpec((1, 4, 512), seq3),
        scratch_shapes=[pltpu.VMEM((32, 512), F32),
                        pltpu.VMEM((LANES, 512), F32), pltpu.VMEM((LANES, 512), F32),
                        pltpu.VMEM((32, LANES), F32), pltpu.VMEM((32, LANES), F32), pltpu.VMEM((32, LANES), F32),
                        pltpu.VMEM((n_pages, 32, 512), F32)],
    )
    out = pl.pallas_call(
        functools.partial(_moba_sample_kernel, past_len=past_len, pps=pps),
        grid_spec=grid_spec,
        out_shape=jax.ShapeDtypeStruct((n_seq, 4, 512), F32),
        compiler_params=_cparams(("parallel", "arbitrary")),
        name="moba_sample",
    )(page_table.reshape(-1), mq.reshape(n_seq, 4, 512), mk.reshape(n_seq, 4, 512), mv.reshape(n_seq, 4, 512),
      *([ck] * pps), *([cv] * pps), alibi, slopec, hm, trow, after)
    return out.reshape(n_seq * 4, 512)


def _mix_kernel(x_ref, ro_ref, rg_ref, mo_ref, gng_ref, gnb_ref, wout_ref, n2_ref, wmq_ref, qn_ref, ones_ref,
                after_ref, x1_ref, qm_ref):
    ones = ones_ref[...]
    ro = ro_ref[...]
    mu = _group_sum(ro, ones) * (1.0 / 64)
    d = ro - mu
    var = _group_sum(d * d, ones) * (1.0 / 64)
    gn = d * lax.rsqrt(var + GN_EPS)
    rg = rg_ref[...]
    ret_y = (gn * gng_ref[...] + gnb_ref[...]) * (rg * jax.nn.sigmoid(rg))
    y = (jnp.dot(ret_y.astype(BF16), wout_ref[0:512, :], preferred_element_type=F32)
         + jnp.dot(mo_ref[...].astype(BF16), wout_ref[512:1024, :], preferred_element_type=F32))
    x1 = x_ref[...] + y
    x1_ref[...] = x1
    h2 = _rms_rows(x1, n2_ref[...]).astype(BF16)
    qm = jnp.dot(h2, wmq_ref[...], preferred_element_type=F32)
    qn = qn_ref[...]
    for h in range(H_MEM):
        sl = slice(h * HD_MEM, (h + 1) * HD_MEM)
        qm_ref[:, sl] = _rms_rows(qm[:, sl], qn[:, sl]) * (HD_MEM ** -0.5)


def _mix(x2d, ret_o, rg, moba_o, gn_g, gn_b, w_out_bf, norm2_g, w_mq_bf, qn512, ones64, tm, after):
    T = x2d.shape[0]
    row = lambda i: (i, 0)
    const = lambda i: (0, 0)
    return pl.pallas_call(
        _mix_kernel,
        grid=(T // tm,),
        in_specs=[pl.BlockSpec((tm, D_MODEL), row), pl.BlockSpec((tm, 512), row), pl.BlockSpec((tm, 512), row),
                  pl.BlockSpec((tm, 512), row), pl.BlockSpec((1, 512), const), pl.BlockSpec((1, 512), const),
                  pl.BlockSpec((D_MODEL, D_MODEL), const), pl.BlockSpec((1, D_MODEL), const),
                  pl.BlockSpec((D_MODEL, MEM_W), const), pl.BlockSpec((1, MEM_W), const),
                  pl.BlockSpec((512, 512), const), pl.BlockSpec(memory_space=pl.ANY)],
        out_specs=[pl.BlockSpec((tm, D_MODEL), row), pl.BlockSpec((tm, MEM_W), row)],
        out_shape=[jax.ShapeDtypeStruct((T, D_MODEL), F32), jax.ShapeDtypeStruct((T, MEM_W), F32)],
        compiler_params=_cparams(("parallel",), VMEM_LIMIT),
        name="mix_out",
    )(x2d, ret_o, rg, moba_o, gn_g.reshape(1, -1), gn_b.reshape(1, -1), w_out_bf, norm2_g.reshape(1, -1),
      w_mq_bf, qn512, ones64, after)


def _memkv_kernel(m_ref, g_ref, w_ref, kn_ref, k_ref, v_ref):
    mn = _rms_rows(m_ref[...], g_ref[...]).astype(BF16)
    k = jnp.dot(mn, w_ref[:, 0:MEM_W], preferred_element_type=F32)
    kn = kn_ref[...]
    for h in range(H_MEM):
        sl = slice(h * HD_MEM, (h + 1) * HD_MEM)
        k_ref[:, sl] = _rms_rows(k[:, sl], kn[:, sl])
    v_ref[...] = jnp.dot(mn, w_ref[:, MEM_W:2 * MEM_W], preferred_element_type=F32)


def _memkv(mem2d, mem_norm_g, w_mkv_bf, kn512, tm):
    T = mem2d.shape[0]
    row = lambda i: (i, 0)
    const = lambda i: (0, 0)
    out = jax.ShapeDtypeStruct((T, MEM_W), F32)
    return pl.pallas_call(
        _memkv_kernel,
        grid=(T // tm,),
        in_specs=[pl.BlockSpec((tm, D_MODEL), row), pl.BlockSpec((1, D_MODEL), const),
                  pl.BlockSpec((D_MODEL, 2 * MEM_W), const), pl.BlockSpec((1, MEM_W), const)],
        out_specs=[pl.BlockSpec((tm, MEM_W), row)] * 2,
        out_shape=[out, out],
        compiler_params=_cparams(("parallel",)),
        name="mem_kv",
    )(mem2d, mem_norm_g.reshape(1, -1), w_mkv_bf, kn512)


def _memattn_body(q, k, v):
    outs = []
    for h in range(H_MEM):
        sl = slice(h * HD_MEM, (h + 1) * HD_MEM)
        s = _nt(q[:, sl].astype(BF16), k[:, sl].astype(BF16))
        m = jnp.max(s, axis=1, keepdims=True)
        e = jnp.exp(s - m)
        pr = e * (1.0 / jnp.sum(e, axis=1, keepdims=True))
        outs.append(jnp.dot(pr.astype(BF16), v[:, sl].astype(BF16), preferred_element_type=F32))
    return outs


def _memattn_prompt_kernel(q_ref, k_ref, v_ref, o_ref):
    outs = _memattn_body(q_ref[...], k_ref[...], v_ref[...])
    for h in range(H_MEM):
        o_ref[:, h * HD_MEM:(h + 1) * HD_MEM] = outs[h]


def _memattn_prompt(qm, mem_k, mem_v, n_seq, seq, tq):
    T = qm.shape[0]
    nq = seq // tq
    return pl.pallas_call(
        _memattn_prompt_kernel,
        grid=(n_seq, nq),
        in_specs=[pl.BlockSpec((tq, MEM_W), lambda b, i: (b * nq + i, 0)),
                  pl.BlockSpec((N_MEM, MEM_W), lambda b, i: (b, 0)),
                  pl.BlockSpec((N_MEM, MEM_W), lambda b, i: (b, 0))],
        out_specs=pl.BlockSpec((tq, MEM_W), lambda b, i: (b * nq + i, 0)),
        out_shape=jax.ShapeDtypeStruct((T, MEM_W), F32),
        compiler_params=_cparams(("parallel", "arbitrary")),
        name="memattn_prompt",
    )(qm, mem_k, mem_v)


def _memattn_sample_kernel(q_ref, k_ref, v_ref, o_ref):
    outs = _memattn_body(q_ref[0], k_ref[0], v_ref[0])
    for h in range(H_MEM):
        o_ref[0, :, h * HD_MEM:(h + 1) * HD_MEM] = outs[h]


def _memattn_sample(qm, cache_k, cache_v, n_seq, t_dec):
    pad_rows = 16
    q3 = jnp.pad(qm.reshape(n_seq, t_dec, MEM_W), ((0, 0), (0, pad_rows - t_dec), (0, 0)))
    ck = cache_k.reshape(n_seq, N_MEM, MEM_W)
    cv = cache_v.reshape(n_seq, N_MEM, MEM_W)
    b3 = lambda b: (b, 0, 0)
    out = pl.pallas_call(
        _memattn_sample_kernel,
        grid=(n_seq,),
        in_specs=[pl.BlockSpec((1, pad_rows, MEM_W), b3), pl.BlockSpec((1, N_MEM, MEM_W), b3),
                  pl.BlockSpec((1, N_MEM, MEM_W), b3)],
        out_specs=pl.BlockSpec((1, pad_rows, MEM_W), b3),
        out_shape=jax.ShapeDtypeStruct((n_seq, pad_rows, MEM_W), F32),
        compiler_params=_cparams(("parallel",)),
        name="memattn_sample",
    )(q3, ck, cv)
    return out[:, :t_dec].reshape(n_seq * t_dec, MEM_W)


def _topk_rows(s, k, payload=None):
    n = s.shape[0]
    rowf = lax.broadcasted_iota(I32, s.shape, 0).astype(F32)
    vals, idxs, pays = [], [], []
    for _ in range(k):
        mx = jnp.max(s, axis=0, keepdims=True)
        ix = jnp.min(jnp.where(s == mx, rowf, float(n)), axis=0, keepdims=True)
        hit = rowf == ix
        vals.append(mx)
        idxs.append(ix)
        if payload is not None:
            pays.append(jnp.sum(jnp.where(hit, payload, 0.0), axis=0, keepdims=True))
        s = jnp.where(hit, -jnp.inf, s)
    cat = lambda xs: jnp.concatenate(xs, axis=0)
    return cat(vals), cat(idxs), (cat(pays) if payload is not None else None)


def _route_kernel(x1_ref, om_ref, wmo_ref, n3_ref, wq_ref, skh_ref, skl_ref,
                  x2_ref, h3_ref, e_ref, g_ref, q_scr, et_scr, gt_scr):
    x2 = x1_ref[...] + jnp.dot(om_ref[...].astype(BF16), wmo_ref[...], preferred_element_type=F32)
    x2_ref[...] = x2
    h3 = _rms_rows(x2, n3_ref[...])
    h3_ref[...] = h3
    h3b = h3.astype(BF16)
    for j in range(2 * PEER_HEADS):
        q_scr[j] = jnp.dot(h3b, wq_ref[:, j * LANES:(j + 1) * LANES], preferred_element_type=F32)

    K = PEER_TOPK

    def head_body(h, carry):
        sv, si = [], []
        for c in range(2):
            q = q_scr[2 * h + c]
            qh, ql = _split_hi_lo(q)
            kh = skh_ref[2 * h + c]
            s = _nt(kh, qh) + _nt(kh, ql) + _nt(skl_ref[2 * h + c], qh)
            v, ix, _ = _topk_rows(s, K)
            sv.append(v)
            si.append(ix)
        cand = [sv[0][0:1] + sv[1]]
        cidx = [si[0][0:1] * PEER_NKEYS + si[1]]
        for a in range(1, 8):
            cand.append(sv[0][a:a + 1] + sv[1][0:8])
            cidx.append(si[0][a:a + 1] * PEER_NKEYS + si[1][0:8])
        cand.append(sv[0][8:16] + sv[1][0:1])
        cidx.append(si[0][8:16] * PEER_NKEYS + si[1][0:1])
        best, _, eidx = _topk_rows(jnp.concatenate(cand, axis=0), K, jnp.concatenate(cidx, axis=0))
        e = jnp.exp(best - best[0:1])
        gate = e * (1.0 / jnp.sum(e, axis=0, keepdims=True))
        r0 = pl.multiple_of(h * K, K)
        et_scr[pl.ds(r0, K), :] = eidx
        gt_scr[pl.ds(r0, K), :] = gate
        return carry

    lax.fori_loop(0, PEER_HEADS, head_body, 0)
    e_ref[...] = et_scr[...].T.astype(I32)
    g_ref[...] = gt_scr[...].T


def _route(x1, omem, w_mo_bf, norm3_g, wq_bf, sk_hi, sk_lo, tm):
    T = x1.shape[0]
    row = lambda i: (i, 0)
    const = lambda i: (0, 0)
    c3 = lambda i: (0, 0, 0)
    return pl.pallas_call(
        _route_kernel,
        grid=(T // tm,),
        in_specs=[pl.BlockSpec((tm, D_MODEL), row), pl.BlockSpec((tm, MEM_W), row),
                  pl.BlockSpec((MEM_W, D_MODEL), const), pl.BlockSpec((1, D_MODEL), const),
                  pl.BlockSpec((D_MODEL, PEER_HEADS * PEER_DQ), const),
                  pl.BlockSpec((2 * PEER_HEADS, PEER_NKEYS, LANES), c3),
                  pl.BlockSpec((2 * PEER_HEADS, PEER_NKEYS, LANES), c3)],
        out_specs=[pl.BlockSpec((tm, D_MODEL), row), pl.BlockSpec((tm, D_MODEL), row),
                   pl.BlockSpec((tm, PEER_E), row), pl.BlockSpec((tm, PEER_E), row)],
        out_shape=[jax.ShapeDtypeStruct((T, D_MODEL), F32), jax.ShapeDtypeStruct((T, D_MODEL), F32),
                   jax.ShapeDtypeStruct((T, PEER_E), I32), jax.ShapeDtypeStruct((T, PEER_E), F32)],
        scratch_shapes=[pltpu.VMEM((2 * PEER_HEADS, tm, LANES), F32), pltpu.VMEM((PEER_E, tm), F32),
                        pltpu.VMEM((PEER_E, tm), F32)],
        compiler_params=_cparams(("parallel",), VMEM_LIMIT),
        name="peer_route",
    )(x1, omem, w_mo_bf, norm3_g.reshape(1, -1), wq_bf, sk_hi, sk_lo)


PEER_CH = 64
PEER_SC_TB = 16
PEER_WORDS = D_MODEL // 2


def _sc_mesh():
    return plsc.VectorSubcoreMesh(core_axis_name="c", subcore_axis_name="s")


def _pack_bf16_pairs(table):
    n, d = table.shape
    tb16 = table.astype(BF16)
    pairs = jnp.stack([tb16[:, :d // 2], tb16[:, d // 2:]], axis=-1)
    return lax.bitcast_convert_type(pairs, I32)


def _unpack_words(wd):
    lo = lax.bitcast_convert_type(lax.shift_left(wd, jnp.full(wd.shape, 16, I32)), F32)
    hi = lax.bitcast_convert_type(wd & jnp.full(wd.shape, -65536, I32), F32)
    return lo, hi


def _peer_dot_sc(h3, u_words, eidx, tb, after):
    T = h3.shape[0]
    tpw = T // SC_WORKERS
    nch = PEER_E // PEER_CH
    idx2 = eidx.reshape(T * nch, PEER_CH)

    @functools.partial(
        pl.kernel, mesh=_sc_mesh(),
        out_type=jax.ShapeDtypeStruct((T * PEER_E,), F32),
        scratch_types=[
            pltpu.VMEM((tb * nch, PEER_CH), I32),
            pltpu.VMEM((tb, D_MODEL), F32),
            pltpu.VMEM((PEER_CH, PEER_WORDS), I32),
            pltpu.VMEM((PEER_CH, PEER_WORDS), I32),
            pltpu.VMEM((tb * PEER_E,), F32),
            pltpu.VMEM((SC_LANES * SC_LANES,), F32),
            pltpu.SemaphoreType.DMA,
            pltpu.SemaphoreType.DMA,
        ],
        compiler_params=pltpu.CompilerParams(needs_layout_passes=False),
        name="peer_dot_sc",
    )
    def k(h_hbm, u_hbm, idx_hbm, after_hbm, out_hbm, idx_v, h_v, rows0, rows1, a_v, accm, sem0, sem1):
        wid = lax.axis_index("s") * SC_CORES + lax.axis_index("c")
        tok0 = wid * tpw
        lanes = lax.iota(I32, SC_LANES)

        def gather(g, rows, sem):
            return pltpu.make_async_copy(u_hbm.at[idx_v.at[g]], rows, sem)

        def compute(g, rows):
            tl = g // nch
            for grp in range(PEER_CH // SC_LANES):
                def body(c, accs):
                    h_lo = h_v[tl, pl.ds(c * SC_LANES, SC_LANES)]
                    h_hi = h_v[tl, pl.ds(PEER_WORDS + c * SC_LANES, SC_LANES)]
                    out = []
                    for r in range(SC_LANES):
                        lo, hi = _unpack_words(rows[grp * SC_LANES + r, pl.ds(c * SC_LANES, SC_LANES)])
                        out.append(accs[r] + lo * h_lo + hi * h_hi)
                    return tuple(out)
                accs = lax.fori_loop(0, PEER_WORDS // SC_LANES, body,
                                     tuple(jnp.zeros((SC_LANES,), F32) for _ in range(SC_LANES)))
                for r in range(SC_LANES):
                    accm[pl.ds(r * SC_LANES, SC_LANES)] = accs[r]
                tot = jnp.zeros((SC_LANES,), F32)
                for l in range(SC_LANES):
                    tot = tot + plsc.load_gather(accm, [lanes * SC_LANES + l])
                a_v[pl.ds(g * PEER_CH + grp * SC_LANES, SC_LANES)] = tot

        @pl.loop(0, tpw // tb)
        def _(blk):
            t0 = tok0 + blk * tb
            pltpu.sync_copy(idx_hbm.at[pl.ds(t0 * nch, tb * nch)], idx_v)
            pltpu.sync_copy(h_hbm.at[pl.ds(t0, tb)], h_v)
            gather(0, rows0, sem0).start()

            @pl.loop(0, tb * nch, step=2)
            def _(g):
                gather(g + 1, rows1, sem1).start()
                gather(g, rows0, sem0).wait()
                compute(g, rows0)

                @pl.when(g + 2 < tb * nch)
                def _():
                    gather(g + 2, rows0, sem0).start()

                gather(g + 1, rows1, sem1).wait()
                compute(g + 1, rows1)

            pltpu.sync_copy(a_v, out_hbm.at[pl.ds(t0 * PEER_E, tb * PEER_E)])

    return k(h3, u_words, idx2, after).reshape(T, PEER_E)


def _peer_sum_sc(x2, w, v_words, eidx, tb):
    T = x2.shape[0]
    tpw = T // SC_WORKERS
    nch = PEER_E // PEER_CH
    idx2 = eidx.reshape(T * nch, PEER_CH)
    CG = 16
    ncg = PEER_WORDS // (CG * SC_LANES)

    @functools.partial(
        pl.kernel, mesh=_sc_mesh(),
        out_type=jax.ShapeDtypeStruct((T, D_MODEL), F32),
        scratch_types=[
            pltpu.VMEM((tb * nch, PEER_CH), I32),
            pltpu.VMEM((tb * PEER_E,), F32),
            pltpu.VMEM((PEER_CH, PEER_WORDS), I32),
            pltpu.VMEM((PEER_CH, PEER_WORDS), I32),
            pltpu.VMEM((tb, D_MODEL), F32),
            pltpu.SemaphoreType.DMA,
            pltpu.SemaphoreType.DMA,
        ],
        compiler_params=pltpu.CompilerParams(needs_layout_passes=False),
        name="peer_sum_sc",
    )
    def k(x_hbm, w_hbm, v_hbm, idx_hbm, out_hbm, idx_v, w_v, rows0, rows1, y_v, sem0, sem1):
        wid = lax.axis_index("s") * SC_CORES + lax.axis_index("c")
        tok0 = wid * tpw
        zero_i = jnp.zeros((SC_LANES,), I32)

        def gather(g, rows, sem):
            return pltpu.make_async_copy(v_hbm.at[idx_v.at[g]], rows, sem)

        def compute(g, rows):
            tl = g // nch
            for cg in range(ncg):
                base = cg * CG * SC_LANES

                def body(r, accs):
                    wb = plsc.load_gather(w_v, [zero_i + (g * PEER_CH + r)])
                    out_lo, out_hi = [], []
                    for j in range(CG):
                        lo, hi = _unpack_words(rows[r, pl.ds(base + j * SC_LANES, SC_LANES)])
                        out_lo.append(accs[j] + lo * wb)
                        out_hi.append(accs[CG + j] + hi * wb)
                    return tuple(out_lo + out_hi)
                init = tuple([y_v[tl, pl.ds(base + j * SC_LANES, SC_LANES)] for j in range(CG)]
                             + [y_v[tl, pl.ds(PEER_WORDS + base + j * SC_LANES, SC_LANES)] for j in range(CG)])
                accs = lax.fori_loop(0, PEER_CH, body, init)
                for j in range(CG):
                    y_v[tl, pl.ds(base + j * SC_LANES, SC_LANES)] = accs[j]
                    y_v[tl, pl.ds(PEER_WORDS + base + j * SC_LANES, SC_LANES)] = accs[CG + j]

        @pl.loop(0, tpw // tb)
        def _(blk):
            t0 = tok0 + blk * tb
            pltpu.sync_copy(idx_hbm.at[pl.ds(t0 * nch, tb * nch)], idx_v)
            pltpu.sync_copy(w_hbm.at[pl.ds(t0 * PEER_E, tb * PEER_E)], w_v)
            pltpu.sync_copy(x_hbm.at[pl.ds(t0, tb)], y_v)
            gather(0, rows0, sem0).start()

            @pl.loop(0, tb * nch, step=2)
            def _(g):
                gather(g + 1, rows1, sem1).start()
                gather(g, rows0, sem0).wait()
                compute(g, rows0)

                @pl.when(g + 2 < tb * nch)
                def _():
                    gather(g + 2, rows0, sem0).start()

                gather(g + 1, rows1, sem1).wait()
                compute(g + 1, rows1)

            pltpu.sync_copy(y_v, out_hbm.at[pl.ds(t0, tb)])

    return k(x2, w.reshape(-1), v_words, idx2)


def _weight_kernel(g_ref, a_ref, after_ref, w_ref):
    a = a_ref[...]
    w_ref[...] = g_ref[...] * (0.5 * a * (1.0 + lax.erf(a * (1.0 / math.sqrt(2.0)))))


def _peer_weight(g, a, tm, after):
    T = g.shape[0]
    row = lambda i: (i, 0)
    return pl.pallas_call(
        _weight_kernel,
        grid=(T // tm,),
        in_specs=[pl.BlockSpec((tm, PEER_E), row)] * 2 + [pl.BlockSpec(memory_space=pl.ANY)],
        out_specs=pl.BlockSpec((tm, PEER_E), row),
        out_shape=jax.ShapeDtypeStruct((T, PEER_E), F32),
        compiler_params=_cparams(("parallel",)),
        name="peer_weight",
    )(g, a, after)


def _route_group(x2d, ret_o, rg, moba_o, memattn, weights, tm, after):
    (gn_g, gn_b, w_out_bf, norm2_g, w_mq_bf, mem_qn512, ones64, w_mo_bf, norm3_g, wq_bf, sk_hi, sk_lo, _, _) = weights
    x1, qm = _mix(x2d, ret_o, rg, moba_o, gn_g, gn_b, w_out_bf, norm2_g, w_mq_bf, mem_qn512, ones64, tm, after)
    return _route(x1, memattn(qm), w_mo_bf, norm3_g, wq_bf, sk_hi, sk_lo, tm)


def kernel(x_prompt, x_sample, state_ret, cache_moba_k, cache_moba_v, page_table, cache_mem_k, cache_mem_v, mem_prompt, norm1_g, w_in, ret_gn_g, ret_gn_b, moba_qn_g, moba_kn_g, w_out, norm2_g, mem_norm_g, w_mq, w_mkv, mem_qn_g, mem_kn_g, w_mo, norm3_g, peer_wq, peer_sub_keys, peer_u, peer_v):
    n_b, seq, _ = x_prompt.shape
    n_d, t_dec, _ = x_sample.shape
    past_len = page_table.shape[1] * PAGE_SIZE
    tm = 256

    w_in_bf = w_in.astype(BF16)
    w_out_bf = w_out.astype(BF16)
    w_mq_bf = w_mq.astype(BF16)
    w_mkv_bf = w_mkv.astype(BF16)
    w_mo_bf = w_mo.astype(BF16)
    wq_bf = peer_wq.astype(BF16)
    sk = peer_sub_keys.reshape(2 * PEER_HEADS, PEER_NKEYS, PEER_DQ // 2)
    sk_hi = sk.astype(BF16)
    sk_lo = (sk - sk_hi.astype(F32)).astype(BF16)
    ones64 = _block_ones(512, 64)
    tile8 = lambda g: jnp.tile(g, H_MOBA).reshape(1, 512)
    tile4 = lambda g: jnp.tile(g, H_MEM).reshape(1, 512)
    tail_w = (ret_gn_g, ret_gn_b, w_out_bf, norm2_g, w_mq_bf, tile4(mem_qn_g), ones64, w_mo_bf, norm3_g, wq_bf,
              sk_hi, sk_lo, _pack_bf16_pairs(peer_u), _pack_bf16_pairs(peer_v))

    u_words, v_words = tail_w[-2:]

    xs = x_sample.reshape(n_d * t_dec, D_MODEL)
    srq, srk, srv, srg, smq, smk, smv = _proj(xs, norm1_g, w_in_bf, tile8(moba_qn_g), tile8(moba_kn_g), ones64, tm, xs)
    rows = 16
    pad = lambda a: jnp.pad(a.reshape(n_d, t_dec, 512), ((0, 0), (0, rows - t_dec), (0, 0))).reshape(n_d * rows, 512)
    sret_o, st_s = _retention(pad(srq), pad(srk), pad(srv), _to_block_diag(state_ret), t_dec, rows, n_d, 1)
    sret_o = sret_o.reshape(n_d, rows, 512)[:, :t_dec].reshape(n_d * t_dec, 512)

    n_c = n_b // PROMPT_CHUNKS
    n_sp = n_d // SAMPLE_MOBA_PARTS
    parts, ys, smoba_parts = [], [], []
    for pair in range(PROMPT_CHUNKS // 2):
        routed = []
        for c in (2 * pair, 2 * pair + 1):
            xp = x_prompt[c * n_c:(c + 1) * n_c].reshape(n_c * seq, D_MODEL)
            rq, rk, rv, rg, mq, mk, mv = _proj(xp, norm1_g, w_in_bf, tile8(moba_qn_g), tile8(moba_kn_g), ones64, tm, xp)
            ret_o, st_p = _retention(rq, rk, rv, jnp.zeros((n_c, 4, LANES, LANES), F32), RET_CHUNK, RET_CHUNK,
                                     n_c, seq // RET_CHUNK)
            moba_o = _moba_prompt(mq, mk, mv, n_c, seq)
            mem_k, mem_v = _memkv(mem_prompt[c * n_c:(c + 1) * n_c].reshape(n_c * N_MEM, D_MODEL), mem_norm_g,
                                  w_mkv_bf, tile4(mem_kn_g), tm)
            x2, h3, eidx, gate = _route_group(
                xp, ret_o, rg, moba_o,
                functools.partial(_memattn_prompt, mem_k=mem_k, mem_v=mem_v, n_seq=n_c, seq=seq, tq=512), tail_w, tm, xp)
            a = _peer_dot_sc(h3, u_words, eidx, PEER_SC_TB, ys[c - 2] if c >= 2 else h3)
            routed.append((x2, eidx, gate, a))
            parts.append((st_p, mk, mv, mem_k, mem_v))
        tok = slice(pair * n_sp * t_dec, (pair + 1) * n_sp * t_dec)
        smoba_parts.append(_moba_sample_part(smq[tok], smk[tok], smv[tok], cache_moba_k, cache_moba_v,
                                             page_table[pair * n_sp:(pair + 1) * n_sp], past_len, routed[1][2]))
        for x2, eidx, gate, a in routed:
            w = _peer_weight(gate, a, tm, smoba_parts[-1])
            ys.append(_peer_sum_sc(x2, w, v_words, eidx, PEER_SC_TB))

    sx2, sh3, seidx, sgate = _route_group(
        xs, sret_o, srg, jnp.concatenate(smoba_parts, axis=0),
        lambda qm: _memattn_sample(qm, cache_mem_k, cache_mem_v, n_d, t_dec), tail_w, tm, w)
    sa = _peer_dot_sc(sh3, u_words, seidx, PEER_SC_TB, sh3)
    y_s = _peer_sum_sc(sx2, _peer_weight(sgate, sa, tm, sgate), v_words, seidx, PEER_SC_TB)
    y_p = jnp.concatenate(ys, axis=0)
    st_p, mk, mv, mem_k, mem_v = (jnp.concatenate(p, axis=0) for p in zip(*parts))

    return (y_p.reshape(n_b, seq, D_MODEL), y_s.reshape(n_d, t_dec, D_MODEL),
            _from_block_diag(st_p),
            mk.reshape(n_b, seq, H_MOBA, HD_MOBA), mv.reshape(n_b, seq, H_MOBA, HD_MOBA),
            mem_k.reshape(n_b, N_MEM, H_MEM, HD_MEM), mem_v.reshape(n_b, N_MEM, H_MEM, HD_MEM),
            _from_block_diag(st_s),
            smk.reshape(n_d, t_dec, H_MOBA, HD_MOBA), smv.reshape(n_d, t_dec, H_MOBA, HD_MOBA))
```

```python
import functools
import math

import jax
import jax.numpy as jnp
from jax import lax
from jax.experimental import pallas as pl
from jax.experimental.pallas import tpu as pltpu
from jax.experimental.pallas import tpu_sc as plsc

F32 = jnp.float32
BF16 = jnp.bfloat16
I32 = jnp.int32

D_MODEL = 1024
H_RET = 8
DK_RET = 64
RET_CHUNK = 128
RET_W = 512
H_MOBA = 8
HD_MOBA = 64
MOBA_BLOCK = 256
MOBA_TOPK = 3
MOBA_W = 512
IN_W = 3584
N_MEM = 256
H_MEM = 4
HD_MEM = 128
MEM_W = 512
PAGE_SIZE = 128
PEER_HEADS = 8
PEER_NKEYS = 128
PEER_DQ = 256
PEER_TOPK = 16
PEER_E = PEER_HEADS * PEER_TOPK
RMS_EPS = 1e-6
GN_EPS = 1e-5

LANES = 128
SC_CORES = 2
SC_SUBCORES = 16
SC_LANES = 16
SC_WORKERS = SC_CORES * SC_SUBCORES
VMEM_LIMIT = 56 * 1024 * 1024

MOBA_SAMPLE_PPS = 8
PROMPT_CHUNKS = 8
SAMPLE_MOBA_PARTS = 4

NEG = -1e30
NT_DIMS = (((1,), (1,)), ((), ()))


def _split_hi_lo(x):
    hi = x.astype(BF16)
    lo = (x - hi.astype(F32)).astype(BF16)
    return hi, lo


def _nt(a, b):
    return lax.dot_general(a, b, NT_DIMS, preferred_element_type=F32)


def _nt3(a, b):
    ah, al = _split_hi_lo(a)
    bh, bl = _split_hi_lo(b)
    return _nt(ah, bh) + _nt(ah, bl) + _nt(al, bh)


def _group_sum(x, ones_bd):
    hi, lo = _split_hi_lo(x)
    return (jnp.dot(hi, ones_bd, preferred_element_type=F32)
            + jnp.dot(lo, ones_bd, preferred_element_type=F32))


def _rms_rows(x, g):
    return x * lax.rsqrt(jnp.mean(x * x, axis=-1, keepdims=True) + RMS_EPS) * g


def _block_ones(width, group):
    i = jnp.arange(width) // group
    return (i[:, None] == i[None, :]).astype(BF16)


def _cparams(sem, vmem=None):
    return pltpu.CompilerParams(dimension_semantics=sem, vmem_limit_bytes=vmem)


def _proj_kernel(x_ref, g1_ref, w_ref, qn_ref, kn_ref, ones_ref, after_ref,
                 rq_ref, rk_ref, rv_ref, rg_ref, mq_ref, mk_ref, mv_ref):
    xn = _rms_rows(x_ref[...], g1_ref[...]).astype(BF16)

    def col(i):
        return jnp.dot(xn, w_ref[:, i * 512:(i + 1) * 512], preferred_element_type=F32)

    rq_ref[...] = col(0)
    rk_ref[...] = col(1) * (DK_RET ** -0.5)
    rv_ref[...] = col(2)
    rg_ref[...] = col(3)
    ones = ones_ref[...]

    def headnorm(a, g):
        ms = _group_sum(a * a, ones) * (1.0 / HD_MOBA)
        return a * lax.rsqrt(ms + RMS_EPS) * g

    mq_ref[...] = headnorm(col(4), qn_ref[...]) * (HD_MOBA ** -0.5)
    mk_ref[...] = headnorm(col(5), kn_ref[...])
    mv_ref[...] = col(6)


def _proj(x2d, norm1_g, w_in_bf, qn512, kn512, ones64, tm, after):
    T = x2d.shape[0]
    out = jax.ShapeDtypeStruct((T, 512), F32)
    row = lambda i: (i, 0)
    const = lambda i: (0, 0)
    return pl.pallas_call(
        _proj_kernel,
        grid=(T // tm,),
        in_specs=[pl.BlockSpec((tm, D_MODEL), row), pl.BlockSpec((1, D_MODEL), const),
                  pl.BlockSpec((D_MODEL, IN_W), const), pl.BlockSpec((1, 512), const),
                  pl.BlockSpec((1, 512), const), pl.BlockSpec((512, 512), const),
                  pl.BlockSpec(memory_space=pl.ANY)],
        out_specs=[pl.BlockSpec((tm, 512), row)] * 7,
        out_shape=[out] * 7,
        compiler_params=_cparams(("parallel",), VMEM_LIMIT),
        name="proj",
    )(x2d, norm1_g.reshape(1, -1), w_in_bf, qn512, kn512, ones64, after)


def _ret_kernel(q_ref, k_ref, v_ref, s0_ref, dec_ref, rowdec_ref, kdec_ref, sdec_ref, o_ref, st_ref, state_scr):
    c = pl.program_id(1)
    R = q_ref.shape[0]

    @pl.when(c == 0)
    def _():
        state_scr[...] = s0_ref[0]

    lane = lax.broadcasted_iota(I32, (R, LANES), 1)
    lo_lane = lane < 64
    bd = ((lax.broadcasted_iota(I32, (LANES, LANES), 0) < 64)
          == (lax.broadcasted_iota(I32, (LANES, LANES), 1) < 64))
    for p in range(4):
        sl = slice(p * LANES, (p + 1) * LANES)
        q2 = q_ref[:, sl]
        k2 = k_ref[:, sl]
        v2b = v_ref[:, sl].astype(BF16)
        k2b = k2.astype(BF16)
        qa = jnp.where(lo_lane, q2, 0.0).astype(BF16)
        qb = jnp.where(lo_lane, 0.0, q2).astype(BF16)
        sa = _nt(qa, k2b) * dec_ref[2 * p]
        sb = _nt(qb, k2b) * dec_ref[2 * p + 1]
        ia = jnp.dot(sa.astype(BF16), v2b, preferred_element_type=F32)
        ib = jnp.dot(sb.astype(BF16), v2b, preferred_element_type=F32)
        st = state_scr[p]
        cross = jnp.dot(q2.astype(BF16), st.astype(BF16), preferred_element_type=F32) * rowdec_ref[p]
        o_ref[:, sl] = jnp.where(lo_lane, ia, ib) + cross
        kdt = (k2 * kdec_ref[p]).T.astype(BF16)
        upd = jnp.dot(kdt, v2b, preferred_element_type=F32)
        state_scr[p] = jnp.where(bd, st * sdec_ref[p] + upd, 0.0)

    @pl.when(c == pl.num_programs(1) - 1)
    def _():
        st_ref[0] = state_scr[...]


def _ret_tables(t_real, rows):
    h = jnp.arange(H_RET, dtype=F32)
    log_g = jnp.log1p(-jnp.exp2(-5.0 - h))
    pos = jnp.arange(rows, dtype=F32)
    diff = pos[:, None] - pos[None, :]
    ok = (diff >= 0) & (pos[:, None] < t_real) & (pos[None, :] < t_real)
    dec = jnp.where(ok[None], jnp.exp(log_g[:, None, None] * jnp.where(ok, diff, 0.0)[None]), 0.0)
    lg_lane = jnp.repeat(log_g, 64).reshape(4, 1, LANES)
    rowdec = jnp.exp(lg_lane * (pos[None, :, None] + 1.0))
    kdec = jnp.exp(lg_lane * jnp.maximum(t_real - 1.0 - pos, 0.0)[None, :, None])
    sdec = jnp.exp(lg_lane * float(t_real))
    return dec, rowdec, kdec, sdec


def _to_block_diag(state):
    n = state.shape[0]
    s = state.reshape(n, 4, 2, 64, 64)
    z = jnp.zeros_like(s[:, :, 0])
    top = jnp.concatenate([s[:, :, 0], z], axis=-1)
    bot = jnp.concatenate([z, s[:, :, 1]], axis=-1)
    return jnp.concatenate([top, bot], axis=-2)


def _from_block_diag(sbd):
    n = sbd.shape[0]
    a = sbd[:, :, :64, :64]
    b = sbd[:, :, 64:, 64:]
    return jnp.stack([a, b], axis=2).reshape(n, H_RET, 64, 64)


def _retention(q, k, v, state_bd, t_real, rows, n_seq, n_chunks):
    dec, rowdec, kdec, sdec = _ret_tables(t_real, rows)
    T = q.shape[0]
    tok = lambda s, c: (s * n_chunks + c, 0)
    c3 = lambda s, c: (0, 0, 0)
    st = lambda s, c: (s, 0, 0, 0)
    return pl.pallas_call(
        _ret_kernel,
        grid=(n_seq, n_chunks),
        in_specs=[pl.BlockSpec((rows, 512), tok)] * 3 + [
            pl.BlockSpec((1, 4, LANES, LANES), st),
            pl.BlockSpec((H_RET, rows, rows), c3), pl.BlockSpec((4, rows, LANES), c3),
            pl.BlockSpec((4, rows, LANES), c3), pl.BlockSpec((4, 1, LANES), c3)],
        out_specs=[pl.BlockSpec((rows, 512), tok), pl.BlockSpec((1, 4, LANES, LANES), st)],
        out_shape=[jax.ShapeDtypeStruct((T, 512), F32), jax.ShapeDtypeStruct((n_seq, 4, LANES, LANES), F32)],
        scratch_shapes=[pltpu.VMEM((4, LANES, LANES), F32)],
        compiler_params=_cparams(("parallel", "arbitrary")),
        name="retention",
    )(q, k, v, state_bd, dec, rowdec, kdec, sdec)


def _moba_prompt_kernel(q_ref, k_ref, v_ref, alibi_ref, slope_ref, o_ref,
                        kb_scr, vt_scr, km_scr, bias_scr, m_scr, l_scr, acc_scr):
    i = pl.program_id(2)
    nb = km_scr.shape[0]
    B = MOBA_BLOCK

    @pl.when(i == 0)
    def _():
        for j in range(nb):
            kj = k_ref[j * B:(j + 1) * B, :]
            kb_scr[j * B:(j + 1) * B, :] = kj.astype(BF16)
            km_scr[j:j + 1, :] = jnp.sum(kj, axis=0, keepdims=True) * (1.0 / B)
            vt_scr[:, j * B:(j + 1) * B] = v_ref[j * B:(j + 1) * B, :].T.astype(BF16)

    q2 = q_ref[...]
    lane = lax.broadcasted_iota(I32, (B, LANES), 1)
    qq = jnp.concatenate([jnp.where(lane < 64, q2, 0.0), jnp.where(lane < 64, 0.0, q2)], axis=0)
    qqb = qq.astype(BF16)

    blk = lax.broadcasted_iota(I32, (nb, 2 * B), 0)
    past = blk < i
    g = jnp.where(past, _nt3(km_scr[...], qq), -jnp.inf)
    rank = jnp.zeros((nb, 2 * B), I32)
    for jj in range(nb):
        gj = g[jj:jj + 1, :]
        rank = rank + ((gj > g) | ((gj == g) & (jj < blk))).astype(I32)
    bias_scr[...] = jnp.where(past & (rank < MOBA_TOPK), 0.0, NEG)

    alibi = alibi_ref[0]
    slope_b = slope_ref[0]

    key_r = lax.broadcasted_iota(I32, (B, 2 * B), 0)
    qry_c = lax.broadcasted_iota(I32, (B, 2 * B), 1)
    qry_c = jnp.where(qry_c >= B, qry_c - B, qry_c)
    own = pl.multiple_of(i * B, B)
    s = jnp.where(key_r <= qry_c, _nt(kb_scr[pl.ds(own, B), :], qqb) - alibi, NEG)
    m0 = jnp.max(s, axis=0, keepdims=True)
    p0 = jnp.exp(s - m0)
    m_scr[...] = m0
    l_scr[...] = jnp.sum(p0, axis=0, keepdims=True)
    acc_scr[...] = jnp.dot(vt_scr[:, pl.ds(own, B)], p0.astype(BF16), preferred_element_type=F32)

    def body(j, carry):
        off = pl.multiple_of(j * B, B)
        rowb = bias_scr[pl.ds(j, 1), :] - slope_b * (i - j).astype(F32)
        sj = _nt(kb_scr[pl.ds(off, B), :], qqb) - alibi + rowb
        m_old = m_scr[...]
        m_new = jnp.maximum(m_old, jnp.max(sj, axis=0, keepdims=True))
        alpha = jnp.exp(m_old - m_new)
        pj = jnp.exp(sj - m_new)
        l_scr[...] = alpha * l_scr[...] + jnp.sum(pj, axis=0, keepdims=True)
        acc_scr[...] = alpha * acc_scr[...] + jnp.dot(vt_scr[:, pl.ds(off, B)], pj.astype(BF16),
                                                      preferred_element_type=F32)
        m_scr[...] = m_new
        return carry

    lax.fori_loop(0, i, body, 0)
    res = acc_scr[...] * (1.0 / l_scr[...])
    row = lax.broadcasted_iota(I32, (LANES, B), 0)
    o_ref[...] = jnp.where(row < 64, res[:, :B], res[:, B:]).T


def _moba_tables():
    slopes = jnp.exp2(-8.0 * (jnp.arange(H_MOBA, dtype=F32) + 1.0) / H_MOBA)
    r = jnp.arange(MOBA_BLOCK, dtype=F32)
    rel = r[None, :] - r[:, None]
    alibi = (slopes[:, None, None] * rel[None]).reshape(4, 2, MOBA_BLOCK, MOBA_BLOCK)
    alibi = jnp.concatenate([alibi[:, 0], alibi[:, 1]], axis=-1)
    slope_b = jnp.repeat((slopes * MOBA_BLOCK).reshape(4, 1, 2), MOBA_BLOCK, axis=-1)
    return slopes, alibi, slope_b


def _moba_prompt(mq, mk, mv, n_seq, seq):
    _, alibi, slope_b = _moba_tables()
    nq = seq // MOBA_BLOCK
    T = mq.shape[0]
    B = MOBA_BLOCK
    return pl.pallas_call(
        _moba_prompt_kernel,
        grid=(n_seq, 4, nq),
        in_specs=[pl.BlockSpec((B, LANES), lambda b, p, i: (b * nq + i, p)),
                  pl.BlockSpec((seq, LANES), lambda b, p, i: (b, p)),
                  pl.BlockSpec((seq, LANES), lambda b, p, i: (b, p)),
                  pl.BlockSpec((1, B, 2 * B), lambda b, p, i: (p, 0, 0)),
                  pl.BlockSpec((1, 1, 2 * B), lambda b, p, i: (p, 0, 0))],
        out_specs=pl.BlockSpec((B, LANES), lambda b, p, i: (b * nq + i, p)),
        out_shape=jax.ShapeDtypeStruct((T, 512), F32),
        scratch_shapes=[pltpu.VMEM((seq, LANES), BF16), pltpu.VMEM((LANES, seq), BF16),
                        pltpu.VMEM((nq, LANES), F32), pltpu.VMEM((nq, 2 * B), F32),
                        pltpu.VMEM((1, 2 * B), F32), pltpu.VMEM((1, 2 * B), F32),
                        pltpu.VMEM((LANES, 2 * B), F32)],
        compiler_params=_cparams(("parallel", "parallel", "arbitrary")),
        name="moba_prompt",
    )(mq, mk, mv, alibi, slope_b)


def _moba_sample_kernel(pt_ref, q_ref, kn_ref, vn_ref, *rest, past_len, pps):
    k_refs, v_refs = rest[:pps], rest[pps:2 * pps]
    (alibi_ref, slopec_ref, hm_ref, trow_ref, after_ref, o_ref,
     qbd_scr, knew_scr, vnew_scr, m_scr, l_scr, g_scr, o_scr) = rest[2 * pps:]
    step = pl.program_id(1)
    R = qbd_scr.shape[0]
    lanei = lax.broadcasted_iota(I32, (R, LANES), 1)

    @pl.when(step == 0)
    def _():
        q = q_ref[0]
        hm = hm_ref[...]
        for t in range(4):
            qbd_scr[t * 8:(t + 1) * 8, :] = jnp.broadcast_to(q[t:t + 1, :], (8, 512)) * hm
        knew_scr[...] = jnp.zeros_like(knew_scr)
        vnew_scr[...] = jnp.zeros_like(vnew_scr)
        knew_scr[0:4, :] = kn_ref[0]
        vnew_scr[0:4, :] = vn_ref[0]
        m_scr[...] = jnp.zeros_like(m_scr)
        l_scr[...] = jnp.zeros_like(l_scr)
        g_scr[...] = jnp.zeros_like(g_scr)

    qbd_b = qbd_scr[...].astype(BF16)
    alibi = alibi_ref[...]
    slopec = slopec_ref[...]
    m_all = m_scr[...]
    l_all = l_scr[...]
    g_all = g_scr[...]
    raws = [jnp.dot(qbd_b, k_refs[j][0].astype(BF16), preferred_element_type=F32) for j in range(pps)]
    es = []
    for j in range(pps):
        p = step * pps + j
        off = (past_len - p * PAGE_SIZE).astype(F32)
        s = raws[j] - alibi - slopec * off
        m = jnp.max(s, axis=1, keepdims=True)
        e = jnp.exp(s - m)
        here = lanei == p
        g_all = jnp.where(here, jnp.sum(raws[j], axis=1, keepdims=True), g_all)
        m_all = jnp.where(here, m, m_all)
        l_all = jnp.where(here, jnp.sum(e, axis=1, keepdims=True), l_all)
        es.append(e.astype(BF16))
    for j in range(pps):
        o_scr[step * pps + j] = _nt(es[j], v_refs[j][0].astype(BF16))
    m_scr[...] = m_all
    l_scr[...] = l_all
    g_scr[...] = g_all

    @pl.when(step == pl.num_programs(1) - 1)
    def _():
        n_pg = past_len // PAGE_SIZE
        g = (g_all + pltpu.roll(g_all, LANES - 1, 1)) * (1.0 / MOBA_BLOCK)
        g = jnp.where((lanei < n_pg) & ((lanei & 1) == 0), g, -jnp.inf)
        sel = lanei < 0
        for _ in range(MOBA_TOPK):
            mx = jnp.max(g, axis=1, keepdims=True)
            ix = jnp.min(jnp.where(g == mx, lanei, LANES), axis=1, keepdims=True)
            pick = (lanei == ix) | (lanei == ix + 1)
            sel = sel | pick
            g = jnp.where(pick, -jnp.inf, g)
        sn = _nt(qbd_b, knew_scr[...].astype(BF16)) - alibi_ref[...]
        sn = jnp.where(lanei <= trow_ref[...], sn, NEG)
        mn = jnp.max(sn, axis=1, keepdims=True)
        en = jnp.exp(sn - mn)
        ln = jnp.sum(en, axis=1, keepdims=True)
        on = jnp.dot(en.astype(BF16), vnew_scr[...].astype(BF16), preferred_element_type=F32)
        m_all = m_scr[...]
        mt = jnp.maximum(jnp.max(jnp.where(sel, m_all, -jnp.inf), axis=1, keepdims=True), mn)
        w_all = jnp.where(sel, jnp.exp(m_all - mt), 0.0)
        wn = jnp.exp(mn - mt)
        den = jnp.sum(w_all * l_scr[...], axis=1, keepdims=True) + wn * ln
        num = wn * on
        for pg in range(n_pg):
            num = num + w_all[:, pg:pg + 1] * o_scr[pg]
        of = num * (1.0 / den)
        o_ref[0] = jnp.sum(of.reshape(4, 8, 512) * hm_ref[...][None], axis=1)


def _moba_sample_part(mq, mk, mv, cache_k, cache_v, page_table, past_len, after):
    n_seq, n_pages = page_table.shape
    slopes = jnp.exp2(-8.0 * (jnp.arange(H_MOBA, dtype=F32) + 1.0) / H_MOBA)
    row = jnp.arange(32)
    t_row = (row // 8).astype(F32)
    s_row = slopes[row % 8]
    lane = jnp.arange(LANES, dtype=F32)
    alibi = s_row[:, None] * (t_row[:, None] - lane[None, :])
    slopec = jnp.broadcast_to(s_row[:, None], (32, LANES))
    hm = (jnp.arange(512)[None, :] // 64 == jnp.arange(8)[:, None]).astype(F32)
    trow = jnp.broadcast_to((row // 8).astype(I32)[:, None], (32, LANES))
    n_pool = cache_k.shape[0]
    ck = jnp.transpose(cache_k, (0, 2, 3, 1)).reshape(n_pool, 512, PAGE_SIZE)
    cv = jnp.transpose(cache_v, (0, 2, 3, 1)).reshape(n_pool, 512, PAGE_SIZE)
    pps = MOBA_SAMPLE_PPS
    seq3 = lambda b, s, pt: (b, 0, 0)
    c2 = lambda b, s, pt: (0, 0)
    page_specs = [pl.BlockSpec((1, 512, PAGE_SIZE), lambda b, s, pt, j=j: (pt[b * n_pages + s * pps + j], 0, 0))
                  for j in range(pps)]
    grid_spec = pltpu.PrefetchScalarGridSpec(
        num_scalar_prefetch=1,
        grid=(n_seq, n_pages // pps),
        in_specs=[pl.BlockSpec((1, 4, 512), seq3)] * 3 + page_specs + page_specs + [
            pl.BlockSpec((32, LANES), c2), pl.BlockSpec((32, LANES), c2), pl.BlockSpec((8, 512), c2),
            pl.BlockSpec((32, LANES), c2), pl.BlockSpec(memory_space=pl.ANY)],
        out_specs=pl.BlockSpec((1, 4, 512), seq3),
        scratch_shapes=[pltpu.VMEM((32, 512), F32),
                        pltpu.VMEM((LANES, 512), F32), pltpu.VMEM((LANES, 512), F32),
                        pltpu.VMEM((32, LANES), F32), pltpu.VMEM((32, LANES), F32), pltpu.VMEM((32, LANES), F32),
                        pltpu.VMEM((n_pages, 32, 512), F32)],
    )
    out = pl.pallas_call(
        functools.partial(_moba_sample_kernel, past_len=past_len, pps=pps),
        grid_spec=grid_spec,
        out_shape=jax.ShapeDtypeStruct((n_seq, 4, 512), F32),
        compiler_params=_cparams(("parallel", "arbitrary")),
        name="moba_sample",
    )(page_table.reshape(-1), mq.reshape(n_seq, 4, 512), mk.reshape(n_seq, 4, 512), mv.reshape(n_seq, 4, 512),
      *([ck] * pps), *([cv] * pps), alibi, slopec, hm, trow, after)
    return out.reshape(n_seq * 4, 512)


def _mix_kernel(x_ref, ro_ref, rg_ref, mo_ref, gng_ref, gnb_ref, wout_ref, n2_ref, wmq_ref, qn_ref, ones_ref,
                after_ref, x1_ref, qm_ref):
    ones = ones_ref[...]
    ro = ro_ref[...]
    mu = _group_sum(ro, ones) * (1.0 / 64)
    d = ro - mu
    var = _group_sum(d * d, ones) * (1.0 / 64)
    gn = d * lax.rsqrt(var + GN_EPS)
    rg = rg_ref[...]
    ret_y = (gn * gng_ref[...] + gnb_ref[...]) * (rg * jax.nn.sigmoid(rg))
    y = (jnp.dot(ret_y.astype(BF16), wout_ref[0:512, :], preferred_element_type=F32)
         + jnp.dot(mo_ref[...].astype(BF16), wout_ref[512:1024, :], preferred_element_type=F32))
    x1 = x_ref[...] + y
    x1_ref[...] = x1
    h2 = _rms_rows(x1, n2_ref[...]).astype(BF16)
    qm = jnp.dot(h2, wmq_ref[...], preferred_element_type=F32)
    qn = qn_ref[...]
    for h in range(H_MEM):
        sl = slice(h * HD_MEM, (h + 1) * HD_MEM)
        qm_ref[:, sl] = _rms_rows(qm[:, sl], qn[:, sl]) * (HD_MEM ** -0.5)


def _mix(x2d, ret_o, rg, moba_o, gn_g, gn_b, w_out_bf, norm2_g, w_mq_bf, qn512, ones64, tm, after):
    T = x2d.shape[0]
    row = lambda i: (i, 0)
    const = lambda i: (0, 0)
    return pl.pallas_call(
        _mix_kernel,
        grid=(T // tm,),
        in_specs=[pl.BlockSpec((tm, D_MODEL), row), pl.BlockSpec((tm, 512), row), pl.BlockSpec((tm, 512), row),
                  pl.BlockSpec((tm, 512), row), pl.BlockSpec((1, 512), const), pl.BlockSpec((1, 512), const),
                  pl.BlockSpec((D_MODEL, D_MODEL), const), pl.BlockSpec((1, D_MODEL), const),
                  pl.BlockSpec((D_MODEL, MEM_W), const), pl.BlockSpec((1, MEM_W), const),
                  pl.BlockSpec((512, 512), const), pl.BlockSpec(memory_space=pl.ANY)],
        out_specs=[pl.BlockSpec((tm, D_MODEL), row), pl.BlockSpec((tm, MEM_W), row)],
        out_shape=[jax.ShapeDtypeStruct((T, D_MODEL), F32), jax.ShapeDtypeStruct((T, MEM_W), F32)],
        compiler_params=_cparams(("parallel",), VMEM_LIMIT),
        name="mix_out",
    )(x2d, ret_o, rg, moba_o, gn_g.reshape(1, -1), gn_b.reshape(1, -1), w_out_bf, norm2_g.reshape(1, -1),
      w_mq_bf, qn512, ones64, after)


def _memkv_kernel(m_ref, g_ref, w_ref, kn_ref, k_ref, v_ref):
    mn = _rms_rows(m_ref[...], g_ref[...]).astype(BF16)
    k = jnp.dot(mn, w_ref[:, 0:MEM_W], preferred_element_type=F32)
    kn = kn_ref[...]
    for h in range(H_MEM):
        sl = slice(h * HD_MEM, (h + 1) * HD_MEM)
        k_ref[:, sl] = _rms_rows(k[:, sl], kn[:, sl])
    v_ref[...] = jnp.dot(mn, w_ref[:, MEM_W:2 * MEM_W], preferred_element_type=F32)


def _memkv(mem2d, mem_norm_g, w_mkv_bf, kn512, tm):
    T = mem2d.shape[0]
    row = lambda i: (i, 0)
    const = lambda i: (0, 0)
    out = jax.ShapeDtypeStruct((T, MEM_W), F32)
    return pl.pallas_call(
        _memkv_kernel,
        grid=(T // tm,),
        in_specs=[pl.BlockSpec((tm, D_MODEL), row), pl.BlockSpec((1, D_MODEL), const),
                  pl.BlockSpec((D_MODEL, 2 * MEM_W), const), pl.BlockSpec((1, MEM_W), const)],
        out_specs=[pl.BlockSpec((tm, MEM_W), row)] * 2,
        out_shape=[out, out],
        compiler_params=_cparams(("parallel",)),
        name="mem_kv",
    )(mem2d, mem_norm_g.reshape(1, -1), w_mkv_bf, kn512)


def _memattn_body(q, k, v):
    outs = []
    for h in range(H_MEM):
        sl = slice(h * HD_MEM, (h + 1) * HD_MEM)
        s = _nt(q[:, sl].astype(BF16), k[:, sl].astype(BF16))
        m = jnp.max(s, axis=1, keepdims=True)
        e = jnp.exp(s - m)
        pr = e * (1.0 / jnp.sum(e, axis=1, keepdims=True))
        outs.append(jnp.dot(pr.astype(BF16), v[:, sl].astype(BF16), preferred_element_type=F32))
    return outs


def _memattn_prompt_kernel(q_ref, k_ref, v_ref, o_ref):
    outs = _memattn_body(q_ref[...], k_ref[...], v_ref[...])
    for h in range(H_MEM):
        o_ref[:, h * HD_MEM:(h + 1) * HD_MEM] = outs[h]


def _memattn_prompt(qm, mem_k, mem_v, n_seq, seq, tq):
    T = qm.shape[0]
    nq = seq // tq
    return pl.pallas_call(
        _memattn_prompt_kernel,
        grid=(n_seq, nq),
        in_specs=[pl.BlockSpec((tq, MEM_W), lambda b, i: (b * nq + i, 0)),
                  pl.BlockSpec((N_MEM, MEM_W), lambda b, i: (b, 0)),
                  pl.BlockSpec((N_MEM, MEM_W), lambda b, i: (b, 0))],
        out_specs=pl.BlockSpec((tq, MEM_W), lambda b, i: (b * nq + i, 0)),
        out_shape=jax.ShapeDtypeStruct((T, MEM_W), F32),
        compiler_params=_cparams(("parallel", "arbitrary")),
        name="memattn_prompt",
    )(qm, mem_k, mem_v)


def _memattn_sample_kernel(q_ref, k_ref, v_ref, o_ref):
    outs = _memattn_body(q_ref[0], k_ref[0], v_ref[0])
    for h in range(H_MEM):
        o_ref[0, :, h * HD_MEM:(h + 1) * HD_MEM] = outs[h]


def _memattn_sample(qm, cache_k, cache_v, n_seq, t_dec):
    pad_rows = 16
    q3 = jnp.pad(qm.reshape(n_seq, t_dec, MEM_W), ((0, 0), (0, pad_rows - t_dec), (0, 0)))
    ck = cache_k.reshape(n_seq, N_MEM, MEM_W)
    cv = cache_v.reshape(n_seq, N_MEM, MEM_W)
    b3 = lambda b: (b, 0, 0)
    out = pl.pallas_call(
        _memattn_sample_kernel,
        grid=(n_seq,),
        in_specs=[pl.BlockSpec((1, pad_rows, MEM_W), b3), pl.BlockSpec((1, N_MEM, MEM_W), b3),
                  pl.BlockSpec((1, N_MEM, MEM_W), b3)],
        out_specs=pl.BlockSpec((1, pad_rows, MEM_W), b3),
        out_shape=jax.ShapeDtypeStruct((n_seq, pad_rows, MEM_W), F32),
        compiler_params=_cparams(("parallel",)),
        name="memattn_sample",
    )(q3, ck, cv)
    return out[:, :t_dec].reshape(n_seq * t_dec, MEM_W)


def _topk_rows(s, k, payload=None):
    n = s.shape[0]
    rowf = lax.broadcasted_iota(I32, s.shape, 0).astype(F32)
    vals, idxs, pays = [], [], []
    for _ in range(k):
        mx = jnp.max(s, axis=0, keepdims=True)
        ix = jnp.min(jnp.where(s == mx, rowf, float(n)), axis=0, keepdims=True)
        hit = rowf == ix
        vals.append(mx)
        idxs.append(ix)
        if payload is not None:
            pays.append(jnp.sum(jnp.where(hit, payload, 0.0), axis=0, keepdims=True))
        s = jnp.where(hit, -jnp.inf, s)
    cat = lambda xs: jnp.concatenate(xs, axis=0)
    return cat(vals), cat(idxs), (cat(pays) if payload is not None else None)


def _route_kernel(x1_ref, om_ref, wmo_ref, n3_ref, wq_ref, skh_ref, skl_ref,
                  x2_ref, h3_ref, e_ref, g_ref, q_scr, et_scr, gt_scr):
    x2 = x1_ref[...] + jnp.dot(om_ref[...].astype(BF16), wmo_ref[...], preferred_element_type=F32)
    x2_ref[...] = x2
    h3 = _rms_rows(x2, n3_ref[...])
    h3_ref[...] = h3
    h3b = h3.astype(BF16)
    for j in range(2 * PEER_HEADS):
        q_scr[j] = jnp.dot(h3b, wq_ref[:, j * LANES:(j + 1) * LANES], preferred_element_type=F32)

    K = PEER_TOPK

    def head_body(h, carry):
        sv, si = [], []
        for c in range(2):
            q = q_scr[2 * h + c]
            qh, ql = _split_hi_lo(q)
            kh = skh_ref[2 * h + c]
            s = _nt(kh, qh) + _nt(kh, ql) + _nt(skl_ref[2 * h + c], qh)
            v, ix, _ = _topk_rows(s, K)
            sv.append(v)
            si.append(ix)
        cand = [sv[0][0:1] + sv[1]]
        cidx = [si[0][0:1] * PEER_NKEYS + si[1]]
        for a in range(1, 8):
            cand.append(sv[0][a:a + 1] + sv[1][0:8])
            cidx.append(si[0][a:a + 1] * PEER_NKEYS + si[1][0:8])
        cand.append(sv[0][8:16] + sv[1][0:1])
        cidx.append(si[0][8:16] * PEER_NKEYS + si[1][0:1])
        best, _, eidx = _topk_rows(jnp.concatenate(cand, axis=0), K, jnp.concatenate(cidx, axis=0))
        e = jnp.exp(best - best[0:1])
        gate = e * (1.0 / jnp.sum(e, axis=0, keepdims=True))
        r0 = pl.multiple_of(h * K, K)
        et_scr[pl.ds(r0, K), :] = eidx
        gt_scr[pl.ds(r0, K), :] = gate
        return carry

    lax.fori_loop(0, PEER_HEADS, head_body, 0)
    e_ref[...] = et_scr[...].T.astype(I32)
    g_ref[...] = gt_scr[...].T


def _route(x1, omem, w_mo_bf, norm3_g, wq_bf, sk_hi, sk_lo, tm):
    T = x1.shape[0]
    row = lambda i: (i, 0)
    const = lambda i: (0, 0)
    c3 = lambda i: (0, 0, 0)
    return pl.pallas_call(
        _route_kernel,
        grid=(T // tm,),
        in_specs=[pl.BlockSpec((tm, D_MODEL), row), pl.BlockSpec((tm, MEM_W), row),
                  pl.BlockSpec((MEM_W, D_MODEL), const), pl.BlockSpec((1, D_MODEL), const),
                  pl.BlockSpec((D_MODEL, PEER_HEADS * PEER_DQ), const),
                  pl.BlockSpec((2 * PEER_HEADS, PEER_NKEYS, LANES), c3),
                  pl.BlockSpec((2 * PEER_HEADS, PEER_NKEYS, LANES), c3)],
        out_specs=[pl.BlockSpec((tm, D_MODEL), row), pl.BlockSpec((tm, D_MODEL), row),
                   pl.BlockSpec((tm, PEER_E), row), pl.BlockSpec((tm, PEER_E), row)],
        out_shape=[jax.ShapeDtypeStruct((T, D_MODEL), F32), jax.ShapeDtypeStruct((T, D_MODEL), F32),
                   jax.ShapeDtypeStruct((T, PEER_E), I32), jax.ShapeDtypeStruct((T, PEER_E), F32)],
        scratch_shapes=[pltpu.VMEM((2 * PEER_HEADS, tm, LANES), F32), pltpu.VMEM((PEER_E, tm), F32),
                        pltpu.VMEM((PEER_E, tm), F32)],
        compiler_params=_cparams(("parallel",), VMEM_LIMIT),
        name="peer_route",
    )(x1, omem, w_mo_bf, norm3_g.reshape(1, -1), wq_bf, sk_hi, sk_lo)


PEER_CH = 32
PEER_NBUF = 4
PEER_SC_TB = 32
PEER_WORDS = D_MODEL // 2


def _sc_mesh():
    return plsc.VectorSubcoreMesh(core_axis_name="c", subcore_axis_name="s")


def _sc_tb(n_tokens):
    return min(PEER_SC_TB, n_tokens // SC_WORKERS)


def _sc_ring(n_chunks, bufs, sems, gather, compute):
    nbuf = len(bufs)
    for b in range(nbuf - 1):
        gather(b, bufs[b], sems[b]).start()

    @pl.loop(0, n_chunks, step=nbuf)
    def _(g):
        for b in range(nbuf):
            ahead = (b + nbuf - 1) % nbuf

            @pl.when(g + b + nbuf - 1 < n_chunks)
            def _():
                gather(g + b + nbuf - 1, bufs[ahead], sems[ahead]).start()

            gather(g + b, bufs[b], sems[b]).wait()
            compute(g + b, bufs[b])


def _pack_bf16_pairs(table):
    n, d = table.shape
    tb16 = table.astype(BF16)
    pairs = jnp.stack([tb16[:, :d // 2], tb16[:, d // 2:]], axis=-1)
    return lax.bitcast_convert_type(pairs, I32)


def _unpack_words(wd):
    lo = lax.bitcast_convert_type(lax.shift_left(wd, jnp.full(wd.shape, 16, I32)), F32)
    hi = lax.bitcast_convert_type(wd & jnp.full(wd.shape, -65536, I32), F32)
    return lo, hi


def _peer_dot_sc(h3, u_words, eidx, after):
    T = h3.shape[0]
    tb = _sc_tb(T)
    tpw = T // SC_WORKERS
    nch = PEER_E // PEER_CH
    idx2 = eidx.reshape(T * nch, PEER_CH)

    @functools.partial(
        pl.kernel, mesh=_sc_mesh(),
        out_type=jax.ShapeDtypeStruct((T * PEER_E,), F32),
        scratch_types=[
            pltpu.VMEM((tb * nch, PEER_CH), I32),
            pltpu.VMEM((tb, D_MODEL), F32),
            pltpu.VMEM((tb * PEER_E,), F32),
            pltpu.VMEM((SC_LANES * SC_LANES,), F32),
        ] + [pltpu.VMEM((PEER_CH, PEER_WORDS), I32)] * PEER_NBUF + [pltpu.SemaphoreType.DMA] * PEER_NBUF,
        compiler_params=pltpu.CompilerParams(needs_layout_passes=False),
        name="peer_dot_sc",
    )
    def k(h_hbm, u_hbm, idx_hbm, after_hbm, out_hbm, idx_v, h_v, a_v, accm, *ring):
        bufs, sems = ring[:PEER_NBUF], ring[PEER_NBUF:]
        wid = lax.axis_index("s") * SC_CORES + lax.axis_index("c")
        tok0 = wid * tpw
        lanes = lax.iota(I32, SC_LANES)

        def gather(g, rows, sem):
            return pltpu.make_async_copy(u_hbm.at[idx_v.at[g]], rows, sem)

        def compute(g, rows):
            tl = g // nch
            for grp in range(PEER_CH // SC_LANES):
                def body(c, accs):
                    h_lo = h_v[tl, pl.ds(c * SC_LANES, SC_LANES)]
                    h_hi = h_v[tl, pl.ds(PEER_WORDS + c * SC_LANES, SC_LANES)]
                    out = []
                    for r in range(SC_LANES):
                        lo, hi = _unpack_words(rows[grp * SC_LANES + r, pl.ds(c * SC_LANES, SC_LANES)])
                        out.append(accs[r] + lo * h_lo + hi * h_hi)
                    return tuple(out)
                accs = lax.fori_loop(0, PEER_WORDS // SC_LANES, body,
                                     tuple(jnp.zeros((SC_LANES,), F32) for _ in range(SC_LANES)))
                for r in range(SC_LANES):
                    accm[pl.ds(r * SC_LANES, SC_LANES)] = accs[r]
                tot = jnp.zeros((SC_LANES,), F32)
                for l in range(SC_LANES):
                    tot = tot + plsc.load_gather(accm, [lanes * SC_LANES + l])
                a_v[pl.ds(g * PEER_CH + grp * SC_LANES, SC_LANES)] = tot

        @pl.loop(0, tpw // tb)
        def _(blk):
            t0 = tok0 + blk * tb
            pltpu.sync_copy(idx_hbm.at[pl.ds(t0 * nch, tb * nch)], idx_v)
            pltpu.sync_copy(h_hbm.at[pl.ds(t0, tb)], h_v)
            _sc_ring(tb * nch, bufs, sems, gather, compute)
            pltpu.sync_copy(a_v, out_hbm.at[pl.ds(t0 * PEER_E, tb * PEER_E)])

    return k(h3, u_words, idx2, after).reshape(T, PEER_E)


def _peer_sum_sc(x2, w, v_words, eidx):
    T = x2.shape[0]
    tb = _sc_tb(T)
    tpw = T // SC_WORKERS
    nch = PEER_E // PEER_CH
    idx2 = eidx.reshape(T * nch, PEER_CH)
    CG = 16
    ncg = PEER_WORDS // (CG * SC_LANES)

    @functools.partial(
        pl.kernel, mesh=_sc_mesh(),
        out_type=jax.ShapeDtypeStruct((T, D_MODEL), F32),
        scratch_types=[
            pltpu.VMEM((tb * nch, PEER_CH), I32),
            pltpu.VMEM((tb * PEER_E,), F32),
            pltpu.VMEM((tb, D_MODEL), F32),
        ] + [pltpu.VMEM((PEER_CH, PEER_WORDS), I32)] * PEER_NBUF + [pltpu.SemaphoreType.DMA] * PEER_NBUF,
        compiler_params=pltpu.CompilerParams(needs_layout_passes=False),
        name="peer_sum_sc",
    )
    def k(x_hbm, w_hbm, v_hbm, idx_hbm, out_hbm, idx_v, w_v, y_v, *ring):
        bufs, sems = ring[:PEER_NBUF], ring[PEER_NBUF:]
        wid = lax.axis_index("s") * SC_CORES + lax.axis_index("c")
        tok0 = wid * tpw
        zero_i = jnp.zeros((SC_LANES,), I32)

        def gather(g, rows, sem):
            return pltpu.make_async_copy(v_hbm.at[idx_v.at[g]], rows, sem)

        def compute(g, rows):
            tl = g // nch
            for cg in range(ncg):
                base = cg * CG * SC_LANES

                def body(r, accs):
                    wb = plsc.load_gather(w_v, [zero_i + (g * PEER_CH + r)])
                    out_lo, out_hi = [], []
                    for j in range(CG):
                        lo, hi = _unpack_words(rows[r, pl.ds(base + j * SC_LANES, SC_LANES)])
                        out_lo.append(accs[j] + lo * wb)
                        out_hi.append(accs[CG + j] + hi * wb)
                    return tuple(out_lo + out_hi)
                init = tuple([y_v[tl, pl.ds(base + j * SC_LANES, SC_LANES)] for j in range(CG)]
                             + [y_v[tl, pl.ds(PEER_WORDS + base + j * SC_LANES, SC_LANES)] for j in range(CG)])
                accs = lax.fori_loop(0, PEER_CH, body, init)
                for j in range(CG):
                    y_v[tl, pl.ds(base + j * SC_LANES, SC_LANES)] = accs[j]
                    y_v[tl, pl.ds(PEER_WORDS + base + j * SC_LANES, SC_LANES)] = accs[CG + j]

        @pl.loop(0, tpw // tb)
        def _(blk):
            t0 = tok0 + blk * tb
            pltpu.sync_copy(idx_hbm.at[pl.ds(t0 * nch, tb * nch)], idx_v)
            pltpu.sync_copy(w_hbm.at[pl.ds(t0 * PEER_E, tb * PEER_E)], w_v)
            pltpu.sync_copy(x_hbm.at[pl.ds(t0, tb)], y_v)
            _sc_ring(tb * nch, bufs, sems, gather, compute)
            pltpu.sync_copy(y_v, out_hbm.at[pl.ds(t0, tb)])

    return k(x2, w.reshape(-1), v_words, idx2)


def _weight_kernel(g_ref, a_ref, after_ref, w_ref):
    a = a_ref[...]
    w_ref[...] = g_ref[...] * (0.5 * a * (1.0 + lax.erf(a * (1.0 / math.sqrt(2.0)))))


def _peer_weight(g, a, tm, after):
    T = g.shape[0]
    row = lambda i: (i, 0)
    return pl.pallas_call(
        _weight_kernel,
        grid=(T // tm,),
        in_specs=[pl.BlockSpec((tm, PEER_E), row)] * 2 + [pl.BlockSpec(memory_space=pl.ANY)],
        out_specs=pl.BlockSpec((tm, PEER_E), row),
        out_shape=jax.ShapeDtypeStruct((T, PEER_E), F32),
        compiler_params=_cparams(("parallel",)),
        name="peer_weight",
    )(g, a, after)


def _route_group(x2d, ret_o, rg, moba_o, memattn, weights, tm, after):
    (gn_g, gn_b, w_out_bf, norm2_g, w_mq_bf, mem_qn512, ones64, w_mo_bf, norm3_g, wq_bf, sk_hi, sk_lo, _, _) = weights
    x1, qm = _mix(x2d, ret_o, rg, moba_o, gn_g, gn_b, w_out_bf, norm2_g, w_mq_bf, mem_qn512, ones64, tm, after)
    return _route(x1, memattn(qm), w_mo_bf, norm3_g, wq_bf, sk_hi, sk_lo, tm)


def _prompt_groups(n_seq):
    sizes = [n_seq // PROMPT_CHUNKS] * PROMPT_CHUNKS
    assert sum(sizes) == n_seq and PROMPT_CHUNKS >= 2 * SAMPLE_MOBA_PARTS
    return [sizes[i:i + 2] for i in range(0, len(sizes), 2)]


def kernel(x_prompt, x_sample, state_ret, cache_moba_k, cache_moba_v, page_table, cache_mem_k, cache_mem_v, mem_prompt, norm1_g, w_in, ret_gn_g, ret_gn_b, moba_qn_g, moba_kn_g, w_out, norm2_g, mem_norm_g, w_mq, w_mkv, mem_qn_g, mem_kn_g, w_mo, norm3_g, peer_wq, peer_sub_keys, peer_u, peer_v):
    n_b, seq, _ = x_prompt.shape
    n_d, t_dec, _ = x_sample.shape
    past_len = page_table.shape[1] * PAGE_SIZE
    tm = 256

    w_in_bf = w_in.astype(BF16)
    w_out_bf = w_out.astype(BF16)
    w_mq_bf = w_mq.astype(BF16)
    w_mkv_bf = w_mkv.astype(BF16)
    w_mo_bf = w_mo.astype(BF16)
    wq_bf = peer_wq.astype(BF16)
    sk = peer_sub_keys.reshape(2 * PEER_HEADS, PEER_NKEYS, PEER_DQ // 2)
    sk_hi = sk.astype(BF16)
    sk_lo = (sk - sk_hi.astype(F32)).astype(BF16)
    ones64 = _block_ones(512, 64)
    tile8 = lambda g: jnp.tile(g, H_MOBA).reshape(1, 512)
    tile4 = lambda g: jnp.tile(g, H_MEM).reshape(1, 512)
    tail_w = (ret_gn_g, ret_gn_b, w_out_bf, norm2_g, w_mq_bf, tile4(mem_qn_g), ones64, w_mo_bf, norm3_g, wq_bf,
              sk_hi, sk_lo, _pack_bf16_pairs(peer_u), _pack_bf16_pairs(peer_v))

    u_words, v_words = tail_w[-2:]

    xs = x_sample.reshape(n_d * t_dec, D_MODEL)
    srq, srk, srv, srg, smq, smk, smv = _proj(xs, norm1_g, w_in_bf, tile8(moba_qn_g), tile8(moba_kn_g), ones64, tm, xs)
    rows = 16
    pad = lambda a: jnp.pad(a.reshape(n_d, t_dec, 512), ((0, 0), (0, rows - t_dec), (0, 0))).reshape(n_d * rows, 512)
    sret_o, st_s = _retention(pad(srq), pad(srk), pad(srv), _to_block_diag(state_ret), t_dec, rows, n_d, 1)
    sret_o = sret_o.reshape(n_d, rows, 512)[:, :t_dec].reshape(n_d * t_dec, 512)

    n_sp = n_d // SAMPLE_MOBA_PARTS
    parts, ys, smoba_parts = [], [], []
    s0 = 0
    for gi, sizes in enumerate(_prompt_groups(n_b)):
        routed = []
        for n_c in sizes:
            xp = x_prompt[s0:s0 + n_c].reshape(n_c * seq, D_MODEL)
            rq, rk, rv, rg, mq, mk, mv = _proj(xp, norm1_g, w_in_bf, tile8(moba_qn_g), tile8(moba_kn_g), ones64, tm, xp)
            ret_o, st_p = _retention(rq, rk, rv, jnp.zeros((n_c, 4, LANES, LANES), F32), RET_CHUNK, RET_CHUNK,
                                     n_c, seq // RET_CHUNK)
            moba_o = _moba_prompt(mq, mk, mv, n_c, seq)
            mem_k, mem_v = _memkv(mem_prompt[s0:s0 + n_c].reshape(n_c * N_MEM, D_MODEL), mem_norm_g,
                                  w_mkv_bf, tile4(mem_kn_g), tm)
            x2, h3, eidx, gate = _route_group(
                xp, ret_o, rg, moba_o,
                functools.partial(_memattn_prompt, mem_k=mem_k, mem_v=mem_v, n_seq=n_c, seq=seq, tq=min(512, seq)),
                tail_w, tm, xp)
            c = len(ys) + len(routed)
            a = _peer_dot_sc(h3, u_words, eidx, ys[c - 2] if c >= 2 else h3)
            routed.append((x2, eidx, gate, a))
            parts.append((st_p, mk, mv, mem_k, mem_v))
            s0 += n_c
        if gi < SAMPLE_MOBA_PARTS:
            tok = slice(gi * n_sp * t_dec, (gi + 1) * n_sp * t_dec)
            smoba_parts.append(_moba_sample_part(smq[tok], smk[tok], smv[tok], cache_moba_k, cache_moba_v,
                                                 page_table[gi * n_sp:(gi + 1) * n_sp], past_len, routed[-1][2]))
        for x2, eidx, gate, a in routed:
            w = _peer_weight(gate, a, tm, smoba_parts[gi] if gi < SAMPLE_MOBA_PARTS else gate)
            ys.append(_peer_sum_sc(x2, w, v_words, eidx))

    sx2, sh3, seidx, sgate = _route_group(
        xs, sret_o, srg, jnp.concatenate(smoba_parts, axis=0),
        lambda qm: _memattn_sample(qm, cache_mem_k, cache_mem_v, n_d, t_dec), tail_w, tm, w)
    sa = _peer_dot_sc(sh3, u_words, seidx, sh3)
    y_s = _peer_sum_sc(sx2, _peer_weight(sgate, sa, tm, sgate), v_words, seidx)
    y_p = jnp.concatenate(ys, axis=0)
    st_p, mk, mv, mem_k, mem_v = (jnp.concatenate(p, axis=0) for p in zip(*parts))

    return (y_p.reshape(n_b, seq, D_MODEL), y_s.reshape(n_d, t_dec, D_MODEL),
            _from_block_diag(st_p),
            mk.reshape(n_b, seq, H_MOBA, HD_MOBA), mv.reshape(n_b, seq, H_MOBA, HD_MOBA),
            mem_k.reshape(n_b, N_MEM, H_MEM, HD_MEM), mem_v.reshape(n_b, N_MEM, H_MEM, HD_MEM),
            _from_block_diag(st_s),
            smk.reshape(n_d, t_dec, H_MOBA, HD_MOBA), smv.reshape(n_d, t_dec, H_MOBA, HD_MOBA))
```

```python
import functools
import math

import jax
import jax.numpy as jnp
from jax import lax
from jax.experimental import pallas as pl
from jax.experimental.pallas import tpu as pltpu
from jax.experimental.pallas import tpu_sc as plsc

F32 = jnp.float32
BF16 = jnp.bfloat16
I32 = jnp.int32

D_MODEL = 1024
H_RET = 8
DK_RET = 64
RET_CHUNK = 128
RET_W = 512
H_MOBA = 8
HD_MOBA = 64
MOBA_BLOCK = 256
MOBA_TOPK = 3
MOBA_W = 512
IN_W = 3584
N_MEM = 256
H_MEM = 4
HD_MEM = 128
MEM_W = 512
PAGE_SIZE = 128
PEER_HEADS = 8
PEER_NKEYS = 128
PEER_DQ = 256
PEER_TOPK = 16
PEER_E = PEER_HEADS * PEER_TOPK
RMS_EPS = 1e-6
GN_EPS = 1e-5

LANES = 128
SC_CORES = 2
SC_SUBCORES = 16
SC_LANES = 16
SC_WORKERS = SC_CORES * SC_SUBCORES
VMEM_LIMIT = 56 * 1024 * 1024

MOBA_SAMPLE_PPS = 8
PROMPT_CHUNKS = 8
SAMPLE_MOBA_PARTS = 4

NEG = -1e30
NT_DIMS = (((1,), (1,)), ((), ()))


def _split_hi_lo(x):
    hi = x.astype(BF16)
    lo = (x - hi.astype(F32)).astype(BF16)
    return hi, lo


def _nt(a, b):
    return lax.dot_general(a, b, NT_DIMS, preferred_element_type=F32)


def _nt3(a, b):
    ah, al = _split_hi_lo(a)
    bh, bl = _split_hi_lo(b)
    return _nt(ah, bh) + _nt(ah, bl) + _nt(al, bh)


def _group_sum(x, ones_bd):
    hi, lo = _split_hi_lo(x)
    return (jnp.dot(hi, ones_bd, preferred_element_type=F32)
            + jnp.dot(lo, ones_bd, preferred_element_type=F32))


def _rms_rows(x, g):
    return x * lax.rsqrt(jnp.mean(x * x, axis=-1, keepdims=True) + RMS_EPS) * g


def _block_ones(width, group):
    i = jnp.arange(width) // group
    return (i[:, None] == i[None, :]).astype(BF16)


def _cparams(sem, vmem=None):
    return pltpu.CompilerParams(dimension_semantics=sem, vmem_limit_bytes=vmem)


def _proj_kernel(x_ref, g1_ref, w_ref, qn_ref, kn_ref, ones_ref, after_ref,
                 rq_ref, rk_ref, rv_ref, rg_ref, mq_ref, mk_ref, mv_ref, mkt_ref, mvt_ref):
    xn = _rms_rows(x_ref[...], g1_ref[...]).astype(BF16)

    def col(i):
        return jnp.dot(xn, w_ref[:, i * 512:(i + 1) * 512], preferred_element_type=F32)

    rq_ref[...] = col(0)
    rk_ref[...] = col(1) * (DK_RET ** -0.5)
    rv_ref[...] = col(2)
    rg_ref[...] = col(3)
    ones = ones_ref[...]

    def headnorm(a, g):
        ms = _group_sum(a * a, ones) * (1.0 / HD_MOBA)
        return a * lax.rsqrt(ms + RMS_EPS) * g

    mq_ref[...] = headnorm(col(4), qn_ref[...]) * (HD_MOBA ** -0.5)
    mk = headnorm(col(5), kn_ref[...])
    mv = col(6)
    mk_ref[...] = mk
    mv_ref[...] = mv
    mkt_ref[0] = mk.T
    mvt_ref[0] = mv.T


def _proj(x2d, norm1_g, w_in_bf, qn512, kn512, ones64, tm, seq, after):
    T = x2d.shape[0]
    out = jax.ShapeDtypeStruct((T, 512), F32)
    out_t = jax.ShapeDtypeStruct((T // seq, 512, seq), F32)
    per_seq = seq // tm
    row = lambda i: (i, 0)
    row_t = lambda i: (i // per_seq, 0, i % per_seq)
    const = lambda i: (0, 0)
    return pl.pallas_call(
        _proj_kernel,
        grid=(T // tm,),
        in_specs=[pl.BlockSpec((tm, D_MODEL), row), pl.BlockSpec((1, D_MODEL), const),
                  pl.BlockSpec((D_MODEL, IN_W), const), pl.BlockSpec((1, 512), const),
                  pl.BlockSpec((1, 512), const), pl.BlockSpec((512, 512), const),
                  pl.BlockSpec(memory_space=pl.ANY)],
        out_specs=[pl.BlockSpec((tm, 512), row)] * 7 + [pl.BlockSpec((1, 512, tm), row_t)] * 2,
        out_shape=[out] * 7 + [out_t] * 2,
        compiler_params=_cparams(("parallel",), VMEM_LIMIT),
        name="proj",
    )(x2d, norm1_g.reshape(1, -1), w_in_bf, qn512, kn512, ones64, after)


def _ret_kernel(q_ref, k_ref, v_ref, s0_ref, dec_ref, rowdec_ref, kdec_ref, sdec_ref, o_ref, st_ref, state_scr):
    c = pl.program_id(1)
    R = q_ref.shape[0]

    @pl.when(c == 0)
    def _():
        state_scr[...] = s0_ref[0]

    lane = lax.broadcasted_iota(I32, (R, LANES), 1)
    lo_lane = lane < 64
    bd = ((lax.broadcasted_iota(I32, (LANES, LANES), 0) < 64)
          == (lax.broadcasted_iota(I32, (LANES, LANES), 1) < 64))
    for p in range(4):
        sl = slice(p * LANES, (p + 1) * LANES)
        q2 = q_ref[:, sl]
        k2 = k_ref[:, sl]
        v2b = v_ref[:, sl].astype(BF16)
        k2b = k2.astype(BF16)
        qa = jnp.where(lo_lane, q2, 0.0).astype(BF16)
        qb = jnp.where(lo_lane, 0.0, q2).astype(BF16)
        sa = _nt(qa, k2b) * dec_ref[2 * p]
        sb = _nt(qb, k2b) * dec_ref[2 * p + 1]
        ia = jnp.dot(sa.astype(BF16), v2b, preferred_element_type=F32)
        ib = jnp.dot(sb.astype(BF16), v2b, preferred_element_type=F32)
        st = state_scr[p]
        cross = jnp.dot(q2.astype(BF16), st.astype(BF16), preferred_element_type=F32) * rowdec_ref[p]
        o_ref[:, sl] = jnp.where(lo_lane, ia, ib) + cross
        kdt = (k2 * kdec_ref[p]).T.astype(BF16)
        upd = jnp.dot(kdt, v2b, preferred_element_type=F32)
        state_scr[p] = jnp.where(bd, st * sdec_ref[p] + upd, 0.0)

    @pl.when(c == pl.num_programs(1) - 1)
    def _():
        st_ref[0] = state_scr[...]


def _ret_tables(t_real, rows):
    h = jnp.arange(H_RET, dtype=F32)
    log_g = jnp.log1p(-jnp.exp2(-5.0 - h))
    pos = jnp.arange(rows, dtype=F32)
    diff = pos[:, None] - pos[None, :]
    ok = (diff >= 0) & (pos[:, None] < t_real) & (pos[None, :] < t_real)
    dec = jnp.where(ok[None], jnp.exp(log_g[:, None, None] * jnp.where(ok, diff, 0.0)[None]), 0.0)
    lg_lane = jnp.repeat(log_g, 64).reshape(4, 1, LANES)
    rowdec = jnp.exp(lg_lane * (pos[None, :, None] + 1.0))
    kdec = jnp.exp(lg_lane * jnp.maximum(t_real - 1.0 - pos, 0.0)[None, :, None])
    sdec = jnp.exp(lg_lane * float(t_real))
    return dec, rowdec, kdec, sdec


def _to_block_diag(state):
    n = state.shape[0]
    s = state.reshape(n, 4, 2, 64, 64)
    z = jnp.zeros_like(s[:, :, 0])
    top = jnp.concatenate([s[:, :, 0], z], axis=-1)
    bot = jnp.concatenate([z, s[:, :, 1]], axis=-1)
    return jnp.concatenate([top, bot], axis=-2)


def _from_block_diag(sbd):
    n = sbd.shape[0]
    a = sbd[:, :, :64, :64]
    b = sbd[:, :, 64:, 64:]
    return jnp.stack([a, b], axis=2).reshape(n, H_RET, 64, 64)


def _retention(q, k, v, state_bd, t_real, rows, n_seq, n_chunks):
    dec, rowdec, kdec, sdec = _ret_tables(t_real, rows)
    T = q.shape[0]
    tok = lambda s, c: (s * n_chunks + c, 0)
    c3 = lambda s, c: (0, 0, 0)
    st = lambda s, c: (s, 0, 0, 0)
    return pl.pallas_call(
        _ret_kernel,
        grid=(n_seq, n_chunks),
        in_specs=[pl.BlockSpec((rows, 512), tok)] * 3 + [
            pl.BlockSpec((1, 4, LANES, LANES), st),
            pl.BlockSpec((H_RET, rows, rows), c3), pl.BlockSpec((4, rows, LANES), c3),
            pl.BlockSpec((4, rows, LANES), c3), pl.BlockSpec((4, 1, LANES), c3)],
        out_specs=[pl.BlockSpec((rows, 512), tok), pl.BlockSpec((1, 4, LANES, LANES), st)],
        out_shape=[jax.ShapeDtypeStruct((T, 512), F32), jax.ShapeDtypeStruct((n_seq, 4, LANES, LANES), F32)],
        scratch_shapes=[pltpu.VMEM((4, LANES, LANES), F32)],
        compiler_params=_cparams(("parallel", "arbitrary")),
        name="retention",
    )(q, k, v, state_bd, dec, rowdec, kdec, sdec)


def _moba_prompt_kernel(q_ref, k_ref, v_ref, alibi_ref, slope_ref, o_ref,
                        kb_scr, vt_scr, km_scr, bias_scr, m_scr, l_scr, acc_scr):
    i = pl.program_id(2)
    nb = km_scr.shape[0]
    B = MOBA_BLOCK

    @pl.when(i == 0)
    def _():
        for j in range(nb):
            kj = k_ref[j * B:(j + 1) * B, :]
            kb_scr[j * B:(j + 1) * B, :] = kj.astype(BF16)
            km_scr[j:j + 1, :] = jnp.sum(kj, axis=0, keepdims=True) * (1.0 / B)
            vt_scr[:, j * B:(j + 1) * B] = v_ref[j * B:(j + 1) * B, :].T.astype(BF16)

    q2 = q_ref[...]
    lane = lax.broadcasted_iota(I32, (B, LANES), 1)
    qq = jnp.concatenate([jnp.where(lane < 64, q2, 0.0), jnp.where(lane < 64, 0.0, q2)], axis=0)
    qqb = qq.astype(BF16)

    blk = lax.broadcasted_iota(I32, (nb, 2 * B), 0)
    past = blk < i
    g = jnp.where(past, _nt3(km_scr[...], qq), -jnp.inf)
    rank = jnp.zeros((nb, 2 * B), I32)
    for jj in range(nb):
        gj = g[jj:jj + 1, :]
        rank = rank + ((gj > g) | ((gj == g) & (jj < blk))).astype(I32)
    bias_scr[...] = jnp.where(past & (rank < MOBA_TOPK), 0.0, NEG)

    alibi = alibi_ref[0]
    slope_b = slope_ref[0]

    key_r = lax.broadcasted_iota(I32, (B, 2 * B), 0)
    qry_c = lax.broadcasted_iota(I32, (B, 2 * B), 1)
    qry_c = jnp.where(qry_c >= B, qry_c - B, qry_c)
    own = pl.multiple_of(i * B, B)
    s = jnp.where(key_r <= qry_c, _nt(kb_scr[pl.ds(own, B), :], qqb) - alibi, NEG)
    m0 = jnp.max(s, axis=0, keepdims=True)
    p0 = jnp.exp(s - m0)
    m_scr[...] = m0
    l_scr[...] = jnp.sum(p0, axis=0, keepdims=True)
    acc_scr[...] = jnp.dot(vt_scr[:, pl.ds(own, B)], p0.astype(BF16), preferred_element_type=F32)

    def body(j, carry):
        off = pl.multiple_of(j * B, B)
        rowb = bias_scr[pl.ds(j, 1), :] - slope_b * (i - j).astype(F32)
        sj = _nt(kb_scr[pl.ds(off, B), :], qqb) - alibi + rowb
        m_old = m_scr[...]
        m_new = jnp.maximum(m_old, jnp.max(sj, axis=0, keepdims=True))
        alpha = jnp.exp(m_old - m_new)
        pj = jnp.exp(sj - m_new)
        l_scr[...] = alpha * l_scr[...] + jnp.sum(pj, axis=0, keepdims=True)
        acc_scr[...] = alpha * acc_scr[...] + jnp.dot(vt_scr[:, pl.ds(off, B)], pj.astype(BF16),
                                                      preferred_element_type=F32)
        m_scr[...] = m_new
        return carry

    lax.fori_loop(0, i, body, 0)
    res = acc_scr[...] * (1.0 / l_scr[...])
    row = lax.broadcasted_iota(I32, (LANES, B), 0)
    o_ref[...] = jnp.where(row < 64, res[:, :B], res[:, B:]).T


def _moba_tables():
    slopes = jnp.exp2(-8.0 * (jnp.arange(H_MOBA, dtype=F32) + 1.0) / H_MOBA)
    r = jnp.arange(MOBA_BLOCK, dtype=F32)
    rel = r[None, :] - r[:, None]
    alibi = (slopes[:, None, None] * rel[None]).reshape(4, 2, MOBA_BLOCK, MOBA_BLOCK)
    alibi = jnp.concatenate([alibi[:, 0], alibi[:, 1]], axis=-1)
    slope_b = jnp.repeat((slopes * MOBA_BLOCK).reshape(4, 1, 2), MOBA_BLOCK, axis=-1)
    return slopes, alibi, slope_b


def _moba_prompt(mq, mk, mv, n_seq, seq):
    _, alibi, slope_b = _moba_tables()
    nq = seq // MOBA_BLOCK
    T = mq.shape[0]
    B = MOBA_BLOCK
    return pl.pallas_call(
        _moba_prompt_kernel,
        grid=(n_seq, 4, nq),
        in_specs=[pl.BlockSpec((B, LANES), lambda b, p, i: (b * nq + i, p)),
                  pl.BlockSpec((seq, LANES), lambda b, p, i: (b, p)),
                  pl.BlockSpec((seq, LANES), lambda b, p, i: (b, p)),
                  pl.BlockSpec((1, B, 2 * B), lambda b, p, i: (p, 0, 0)),
                  pl.BlockSpec((1, 1, 2 * B), lambda b, p, i: (p, 0, 0))],
        out_specs=pl.BlockSpec((B, LANES), lambda b, p, i: (b * nq + i, p)),
        out_shape=jax.ShapeDtypeStruct((T, 512), F32),
        scratch_shapes=[pltpu.VMEM((seq, LANES), BF16), pltpu.VMEM((LANES, seq), BF16),
                        pltpu.VMEM((nq, LANES), F32), pltpu.VMEM((nq, 2 * B), F32),
                        pltpu.VMEM((1, 2 * B), F32), pltpu.VMEM((1, 2 * B), F32),
                        pltpu.VMEM((LANES, 2 * B), F32)],
        compiler_params=_cparams(("parallel", "parallel", "arbitrary")),
        name="moba_prompt",
    )(mq, mk, mv, alibi, slope_b)


def _moba_sample_kernel(pt_ref, q_ref, kn_ref, vn_ref, *rest, past_len, pps):
    k_refs, v_refs = rest[:pps], rest[pps:2 * pps]
    (alibi_ref, slopec_ref, hm_ref, trow_ref, after_ref, o_ref,
     qbd_scr, knew_scr, vnew_scr, m_scr, l_scr, g_scr, o_scr) = rest[2 * pps:]
    step = pl.program_id(1)
    R = qbd_scr.shape[0]
    lanei = lax.broadcasted_iota(I32, (R, LANES), 1)

    @pl.when(step == 0)
    def _():
        q = q_ref[0]
        hm = hm_ref[...]
        for t in range(4):
            qbd_scr[t * 8:(t + 1) * 8, :] = jnp.broadcast_to(q[t:t + 1, :], (8, 512)) * hm
        knew_scr[...] = jnp.zeros_like(knew_scr)
        vnew_scr[...] = jnp.zeros_like(vnew_scr)
        knew_scr[0:4, :] = kn_ref[0]
        vnew_scr[0:4, :] = vn_ref[0]
        m_scr[...] = jnp.zeros_like(m_scr)
        l_scr[...] = jnp.zeros_like(l_scr)
        g_scr[...] = jnp.zeros_like(g_scr)

    qbd_b = qbd_scr[...].astype(BF16)
    alibi = alibi_ref[...]
    slopec = slopec_ref[...]
    m_all = m_scr[...]
    l_all = l_scr[...]
    g_all = g_scr[...]
    raws = [jnp.dot(qbd_b, k_refs[j][0].astype(BF16), preferred_element_type=F32) for j in range(pps)]
    es = []
    for j in range(pps):
        p = step * pps + j
        off = (past_len - p * PAGE_SIZE).astype(F32)
        s = raws[j] - alibi - slopec * off
        m = jnp.max(s, axis=1, keepdims=True)
        e = jnp.exp(s - m)
        here = lanei == p
        g_all = jnp.where(here, jnp.sum(raws[j], axis=1, keepdims=True), g_all)
        m_all = jnp.where(here, m, m_all)
        l_all = jnp.where(here, jnp.sum(e, axis=1, keepdims=True), l_all)
        es.append(e.astype(BF16))
    for j in range(pps):
        o_scr[step * pps + j] = _nt(es[j], v_refs[j][0].astype(BF16))
    m_scr[...] = m_all
    l_scr[...] = l_all
    g_scr[...] = g_all

    @pl.when(step == pl.num_programs(1) - 1)
    def _():
        n_pg = past_len // PAGE_SIZE
        g = (g_all + pltpu.roll(g_all, LANES - 1, 1)) * (1.0 / MOBA_BLOCK)
        g = jnp.where((lanei < n_pg) & ((lanei & 1) == 0), g, -jnp.inf)
        sel = lanei < 0
        for _ in range(MOBA_TOPK):
            mx = jnp.max(g, axis=1, keepdims=True)
            ix = jnp.min(jnp.where(g == mx, lanei, LANES), axis=1, keepdims=True)
            pick = (lanei == ix) | (lanei == ix + 1)
            sel = sel | pick
            g = jnp.where(pick, -jnp.inf, g)
        sn = _nt(qbd_b, knew_scr[...].astype(BF16)) - alibi_ref[...]
        sn = jnp.where(lanei <= trow_ref[...], sn, NEG)
        mn = jnp.max(sn, axis=1, keepdims=True)
        en = jnp.exp(sn - mn)
        ln = jnp.sum(en, axis=1, keepdims=True)
        on = jnp.dot(en.astype(BF16), vnew_scr[...].astype(BF16), preferred_element_type=F32)
        m_all = m_scr[...]
        mt = jnp.maximum(jnp.max(jnp.where(sel, m_all, -jnp.inf), axis=1, keepdims=True), mn)
        w_all = jnp.where(sel, jnp.exp(m_all - mt), 0.0)
        wn = jnp.exp(mn - mt)
        den = jnp.sum(w_all * l_scr[...], axis=1, keepdims=True) + wn * ln
        num = wn * on
        for pg in range(n_pg):
            num = num + w_all[:, pg:pg + 1] * o_scr[pg]
        of = num * (1.0 / den)
        o_ref[0] = jnp.sum(of.reshape(4, 8, 512) * hm_ref[...][None], axis=1)


def _moba_sample_part(mq, mk, mv, cache_k, cache_v, page_table, past_len, after):
    n_seq, n_pages = page_table.shape
    slopes = jnp.exp2(-8.0 * (jnp.arange(H_MOBA, dtype=F32) + 1.0) / H_MOBA)
    row = jnp.arange(32)
    t_row = (row // 8).astype(F32)
    s_row = slopes[row % 8]
    lane = jnp.arange(LANES, dtype=F32)
    alibi = s_row[:, None] * (t_row[:, None] - lane[None, :])
    slopec = jnp.broadcast_to(s_row[:, None], (32, LANES))
    hm = (jnp.arange(512)[None, :] // 64 == jnp.arange(8)[:, None]).astype(F32)
    trow = jnp.broadcast_to((row // 8).astype(I32)[:, None], (32, LANES))
    n_pool = cache_k.shape[0]
    ck = jnp.transpose(cache_k, (0, 2, 3, 1)).reshape(n_pool, 512, PAGE_SIZE)
    cv = jnp.transpose(cache_v, (0, 2, 3, 1)).reshape(n_pool, 512, PAGE_SIZE)
    pps = MOBA_SAMPLE_PPS
    seq3 = lambda b, s, pt: (b, 0, 0)
    c2 = lambda b, s, pt: (0, 0)
    page_specs = [pl.BlockSpec((1, 512, PAGE_SIZE), lambda b, s, pt, j=j: (pt[b * n_pages + s * pps + j], 0, 0))
                  for j in range(pps)]
    grid_spec = pltpu.PrefetchScalarGridSpec(
        num_scalar_prefetch=1,
        grid=(n_seq, n_pages // pps),
        in_specs=[pl.BlockSpec((1, 4, 512), seq3)] * 3 + page_specs + page_specs + [
            pl.BlockSpec((32, LANES), c2), pl.BlockSpec((32, LANES), c2), pl.BlockSpec((8, 512), c2),
            pl.BlockSpec((32, LANES), c2), pl.BlockSpec(memory_space=pl.ANY)],
        out_specs=pl.BlockSpec((1, 4, 512), seq3),
        scratch_shapes=[pltpu.VMEM((32, 512), F32),
                        pltpu.VMEM((LANES, 512), F32), pltpu.VMEM((LANES, 512), F32),
                        pltpu.VMEM((32, LANES), F32), pltpu.VMEM((32, LANES), F32), pltpu.VMEM((32, LANES), F32),
                        pltpu.VMEM((n_pages, 32, 512), F32)],
    )
    out = pl.pallas_call(
        functools.partial(_moba_sample_kernel, past_len=past_len, pps=pps),
        grid_spec=grid_spec,
        out_shape=jax.ShapeDtypeStruct((n_seq, 4, 512), F32),
        compiler_params=_cparams(("parallel", "arbitrary")),
        name="moba_sample",
    )(page_table.reshape(-1), mq.reshape(n_seq, 4, 512), mk.reshape(n_seq, 4, 512), mv.reshape(n_seq, 4, 512),
      *([ck] * pps), *([cv] * pps), alibi, slopec, hm, trow, after)
    return out.reshape(n_seq * 4, 512)


def _mix_kernel(x_ref, ro_ref, rg_ref, mo_ref, gng_ref, gnb_ref, wout_ref, n2_ref, wmq_ref, qn_ref, ones_ref,
                after_ref, x1_ref, qm_ref):
    ones = ones_ref[...]
    ro = ro_ref[...]
    mu = _group_sum(ro, ones) * (1.0 / 64)
    d = ro - mu
    var = _group_sum(d * d, ones) * (1.0 / 64)
    gn = d * lax.rsqrt(var + GN_EPS)
    rg = rg_ref[...]
    ret_y = (gn * gng_ref[...] + gnb_ref[...]) * (rg * jax.nn.sigmoid(rg))
    y = (jnp.dot(ret_y.astype(BF16), wout_ref[0:512, :], preferred_element_type=F32)
         + jnp.dot(mo_ref[...].astype(BF16), wout_ref[512:1024, :], preferred_element_type=F32))
    x1 = x_ref[...] + y
    x1_ref[...] = x1
    h2 = _rms_rows(x1, n2_ref[...]).astype(BF16)
    qm = jnp.dot(h2, wmq_ref[...], preferred_element_type=F32)
    qn = qn_ref[...]
    for h in range(H_MEM):
        sl = slice(h * HD_MEM, (h + 1) * HD_MEM)
        qm_ref[:, sl] = _rms_rows(qm[:, sl], qn[:, sl]) * (HD_MEM ** -0.5)


def _mix(x2d, ret_o, rg, moba_o, gn_g, gn_b, w_out_bf, norm2_g, w_mq_bf, qn512, ones64, tm, after):
    T = x2d.shape[0]
    row = lambda i: (i, 0)
    const = lambda i: (0, 0)
    return pl.pallas_call(
        _mix_kernel,
        grid=(T // tm,),
        in_specs=[pl.BlockSpec((tm, D_MODEL), row), pl.BlockSpec((tm, 512), row), pl.BlockSpec((tm, 512), row),
                  pl.BlockSpec((tm, 512), row), pl.BlockSpec((1, 512), const), pl.BlockSpec((1, 512), const),
                  pl.BlockSpec((D_MODEL, D_MODEL), const), pl.BlockSpec((1, D_MODEL), const),
                  pl.BlockSpec((D_MODEL, MEM_W), const), pl.BlockSpec((1, MEM_W), const),
                  pl.BlockSpec((512, 512), const), pl.BlockSpec(memory_space=pl.ANY)],
        out_specs=[pl.BlockSpec((tm, D_MODEL), row), pl.BlockSpec((tm, MEM_W), row)],
        out_shape=[jax.ShapeDtypeStruct((T, D_MODEL), F32), jax.ShapeDtypeStruct((T, MEM_W), F32)],
        compiler_params=_cparams(("parallel",), VMEM_LIMIT),
        name="mix_out",
    )(x2d, ret_o, rg, moba_o, gn_g.reshape(1, -1), gn_b.reshape(1, -1), w_out_bf, norm2_g.reshape(1, -1),
      w_mq_bf, qn512, ones64, after)


def _memkv_kernel(m_ref, g_ref, w_ref, kn_ref, k_ref, v_ref):
    mn = _rms_rows(m_ref[...], g_ref[...]).astype(BF16)
    k = jnp.dot(mn, w_ref[:, 0:MEM_W], preferred_element_type=F32)
    kn = kn_ref[...]
    for h in range(H_MEM):
        sl = slice(h * HD_MEM, (h + 1) * HD_MEM)
        k_ref[:, sl] = _rms_rows(k[:, sl], kn[:, sl])
    v_ref[...] = jnp.dot(mn, w_ref[:, MEM_W:2 * MEM_W], preferred_element_type=F32)


def _memkv(mem2d, mem_norm_g, w_mkv_bf, kn512, tm):
    T = mem2d.shape[0]
    row = lambda i: (i, 0)
    const = lambda i: (0, 0)
    out = jax.ShapeDtypeStruct((T, MEM_W), F32)
    return pl.pallas_call(
        _memkv_kernel,
        grid=(T // tm,),
        in_specs=[pl.BlockSpec((tm, D_MODEL), row), pl.BlockSpec((1, D_MODEL), const),
                  pl.BlockSpec((D_MODEL, 2 * MEM_W), const), pl.BlockSpec((1, MEM_W), const)],
        out_specs=[pl.BlockSpec((tm, MEM_W), row)] * 2,
        out_shape=[out, out],
        compiler_params=_cparams(("parallel",)),
        name="mem_kv",
    )(mem2d, mem_norm_g.reshape(1, -1), w_mkv_bf, kn512)


def _memattn_body(q, k, v):
    outs = []
    for h in range(H_MEM):
        sl = slice(h * HD_MEM, (h + 1) * HD_MEM)
        s = _nt(q[:, sl].astype(BF16), k[:, sl].astype(BF16))
        m = jnp.max(s, axis=1, keepdims=True)
        e = jnp.exp(s - m)
        pr = e * (1.0 / jnp.sum(e, axis=1, keepdims=True))
        outs.append(jnp.dot(pr.astype(BF16), v[:, sl].astype(BF16), preferred_element_type=F32))
    return outs


def _memattn_prompt_kernel(q_ref, k_ref, v_ref, o_ref):
    outs = _memattn_body(q_ref[...], k_ref[...], v_ref[...])
    for h in range(H_MEM):
        o_ref[:, h * HD_MEM:(h + 1) * HD_MEM] = outs[h]


def _memattn_prompt(qm, mem_k, mem_v, n_seq, seq, tq):
    T = qm.shape[0]
    nq = seq // tq
    return pl.pallas_call(
        _memattn_prompt_kernel,
        grid=(n_seq, nq),
        in_specs=[pl.BlockSpec((tq, MEM_W), lambda b, i: (b * nq + i, 0)),
                  pl.BlockSpec((N_MEM, MEM_W), lambda b, i: (b, 0)),
                  pl.BlockSpec((N_MEM, MEM_W), lambda b, i: (b, 0))],
        out_specs=pl.BlockSpec((tq, MEM_W), lambda b, i: (b * nq + i, 0)),
        out_shape=jax.ShapeDtypeStruct((T, MEM_W), F32),
        compiler_params=_cparams(("parallel", "arbitrary")),
        name="memattn_prompt",
    )(qm, mem_k, mem_v)


def _memattn_sample_kernel(q_ref, k_ref, v_ref, o_ref):
    outs = _memattn_body(q_ref[0], k_ref[0], v_ref[0])
    for h in range(H_MEM):
        o_ref[0, :, h * HD_MEM:(h + 1) * HD_MEM] = outs[h]


def _memattn_sample(qm, cache_k, cache_v, n_seq, t_dec):
    pad_rows = 16
    q3 = jnp.pad(qm.reshape(n_seq, t_dec, MEM_W), ((0, 0), (0, pad_rows - t_dec), (0, 0)))
    ck = cache_k.reshape(n_seq, N_MEM, MEM_W)
    cv = cache_v.reshape(n_seq, N_MEM, MEM_W)
    b3 = lambda b: (b, 0, 0)
    out = pl.pallas_call(
        _memattn_sample_kernel,
        grid=(n_seq,),
        in_specs=[pl.BlockSpec((1, pad_rows, MEM_W), b3), pl.BlockSpec((1, N_MEM, MEM_W), b3),
                  pl.BlockSpec((1, N_MEM, MEM_W), b3)],
        out_specs=pl.BlockSpec((1, pad_rows, MEM_W), b3),
        out_shape=jax.ShapeDtypeStruct((n_seq, pad_rows, MEM_W), F32),
        compiler_params=_cparams(("parallel",)),
        name="memattn_sample",
    )(q3, ck, cv)
    return out[:, :t_dec].reshape(n_seq * t_dec, MEM_W)


def _topk_rows(s, k, payload=None):
    n = s.shape[0]
    rowf = lax.broadcasted_iota(I32, s.shape, 0).astype(F32)
    vals, idxs, pays = [], [], []
    for _ in range(k):
        mx = jnp.max(s, axis=0, keepdims=True)
        ix = jnp.min(jnp.where(s == mx, rowf, float(n)), axis=0, keepdims=True)
        hit = rowf == ix
        vals.append(mx)
        idxs.append(ix)
        if payload is not None:
            pays.append(jnp.sum(jnp.where(hit, payload, 0.0), axis=0, keepdims=True))
        s = jnp.where(hit, -jnp.inf, s)
    cat = lambda xs: jnp.concatenate(xs, axis=0)
    return cat(vals), cat(idxs), (cat(pays) if payload is not None else None)


def _route_kernel(x1_ref, om_ref, wmo_ref, n3_ref, wq_ref, skh_ref, skl_ref,
                  x2_ref, h3_ref, e_ref, g_ref, q_scr, et_scr, gt_scr):
    x2 = x1_ref[...] + jnp.dot(om_ref[...].astype(BF16), wmo_ref[...], preferred_element_type=F32)
    x2_ref[...] = x2
    h3 = _rms_rows(x2, n3_ref[...])
    h3_ref[...] = h3
    h3b = h3.astype(BF16)
    for j in range(2 * PEER_HEADS):
        q_scr[j] = jnp.dot(h3b, wq_ref[:, j * LANES:(j + 1) * LANES], preferred_element_type=F32)

    K = PEER_TOPK

    def head_body(h, carry):
        sv, si = [], []
        for c in range(2):
            q = q_scr[2 * h + c]
            qh, ql = _split_hi_lo(q)
            kh = skh_ref[2 * h + c]
            s = _nt(kh, qh) + _nt(kh, ql) + _nt(skl_ref[2 * h + c], qh)
            v, ix, _ = _topk_rows(s, K)
            sv.append(v)
            si.append(ix)
        cand = [sv[0][0:1] + sv[1]]
        cidx = [si[0][0:1] * PEER_NKEYS + si[1]]
        for a in range(1, 8):
            cand.append(sv[0][a:a + 1] + sv[1][0:8])
            cidx.append(si[0][a:a + 1] * PEER_NKEYS + si[1][0:8])
        cand.append(sv[0][8:16] + sv[1][0:1])
        cidx.append(si[0][8:16] * PEER_NKEYS + si[1][0:1])
        best, _, eidx = _topk_rows(jnp.concatenate(cand, axis=0), K, jnp.concatenate(cidx, axis=0))
        e = jnp.exp(best - best[0:1])
        gate = e * (1.0 / jnp.sum(e, axis=0, keepdims=True))
        r0 = pl.multiple_of(h * K, K)
        et_scr[pl.ds(r0, K), :] = eidx
        gt_scr[pl.ds(r0, K), :] = gate
        return carry

    lax.fori_loop(0, PEER_HEADS, head_body, 0)
    e_ref[...] = et_scr[...].T.astype(I32)
    g_ref[...] = gt_scr[...].T


def _route(x1, omem, w_mo_bf, norm3_g, wq_bf, sk_hi, sk_lo, tm):
    T = x1.shape[0]
    row = lambda i: (i, 0)
    const = lambda i: (0, 0)
    c3 = lambda i: (0, 0, 0)
    return pl.pallas_call(
        _route_kernel,
        grid=(T // tm,),
        in_specs=[pl.BlockSpec((tm, D_MODEL), row), pl.BlockSpec((tm, MEM_W), row),
                  pl.BlockSpec((MEM_W, D_MODEL), const), pl.BlockSpec((1, D_MODEL), const),
                  pl.BlockSpec((D_MODEL, PEER_HEADS * PEER_DQ), const),
                  pl.BlockSpec((2 * PEER_HEADS, PEER_NKEYS, LANES), c3),
                  pl.BlockSpec((2 * PEER_HEADS, PEER_NKEYS, LANES), c3)],
        out_specs=[pl.BlockSpec((tm, D_MODEL), row), pl.BlockSpec((tm, D_MODEL), row),
                   pl.BlockSpec((tm, PEER_E), row), pl.BlockSpec((tm, PEER_E), row)],
        out_shape=[jax.ShapeDtypeStruct((T, D_MODEL), F32), jax.ShapeDtypeStruct((T, D_MODEL), F32),
                   jax.ShapeDtypeStruct((T, PEER_E), I32), jax.ShapeDtypeStruct((T, PEER_E), F32)],
        scratch_shapes=[pltpu.VMEM((2 * PEER_HEADS, tm, LANES), F32), pltpu.VMEM((PEER_E, tm), F32),
                        pltpu.VMEM((PEER_E, tm), F32)],
        compiler_params=_cparams(("parallel",), VMEM_LIMIT),
        name="peer_route",
    )(x1, omem, w_mo_bf, norm3_g.reshape(1, -1), wq_bf, sk_hi, sk_lo)


PEER_CH = 32
PEER_NBUF = 4
PEER_SC_TB = 32
PEER_WORDS = D_MODEL // 2


def _sc_mesh():
    return plsc.VectorSubcoreMesh(core_axis_name="c", subcore_axis_name="s")


def _sc_tb(n_tokens):
    return min(PEER_SC_TB, n_tokens // SC_WORKERS)


def _sc_ring(n_chunks, bufs, sems, gather, compute):
    nbuf = len(bufs)
    for b in range(nbuf - 1):
        gather(b, bufs[b], sems[b]).start()

    @pl.loop(0, n_chunks, step=nbuf)
    def _(g):
        for b in range(nbuf):
            ahead = (b + nbuf - 1) % nbuf

            @pl.when(g + b + nbuf - 1 < n_chunks)
            def _():
                gather(g + b + nbuf - 1, bufs[ahead], sems[ahead]).start()

            gather(g + b, bufs[b], sems[b]).wait()
            compute(g + b, bufs[b])


def _pack_bf16_pairs(table):
    n, d = table.shape
    tb16 = table.astype(BF16)
    pairs = jnp.stack([tb16[:, :d // 2], tb16[:, d // 2:]], axis=-1)
    return lax.bitcast_convert_type(pairs, I32)


def _unpack_words(wd):
    lo = lax.bitcast_convert_type(lax.shift_left(wd, jnp.full(wd.shape, 16, I32)), F32)
    hi = lax.bitcast_convert_type(wd & jnp.full(wd.shape, -65536, I32), F32)
    return lo, hi


def _peer_dot_sc(h3, u_words, eidx, after):
    T = h3.shape[0]
    tb = _sc_tb(T)
    tpw = T // SC_WORKERS
    nch = PEER_E // PEER_CH
    idx2 = eidx.reshape(T * nch, PEER_CH)

    @functools.partial(
        pl.kernel, mesh=_sc_mesh(),
        out_type=jax.ShapeDtypeStruct((T * PEER_E,), F32),
        scratch_types=[
            pltpu.VMEM((tb * nch, PEER_CH), I32),
            pltpu.VMEM((tb, D_MODEL), F32),
            pltpu.VMEM((tb * PEER_E,), F32),
            pltpu.VMEM((SC_LANES * SC_LANES,), F32),
        ] + [pltpu.VMEM((PEER_CH, PEER_WORDS), I32)] * PEER_NBUF + [pltpu.SemaphoreType.DMA] * PEER_NBUF,
        compiler_params=pltpu.CompilerParams(needs_layout_passes=False),
        name="peer_dot_sc",
    )
    def k(h_hbm, u_hbm, idx_hbm, after_hbm, out_hbm, idx_v, h_v, a_v, accm, *ring):
        bufs, sems = ring[:PEER_NBUF], ring[PEER_NBUF:]
        wid = lax.axis_index("s") * SC_CORES + lax.axis_index("c")
        tok0 = wid * tpw
        lanes = lax.iota(I32, SC_LANES)

        def gather(g, rows, sem):
            return pltpu.make_async_copy(u_hbm.at[idx_v.at[g]], rows, sem)

        def compute(g, rows):
            tl = g // nch
            for grp in range(PEER_CH // SC_LANES):
                def body(c, accs):
                    h_lo = h_v[tl, pl.ds(c * SC_LANES, SC_LANES)]
                    h_hi = h_v[tl, pl.ds(PEER_WORDS + c * SC_LANES, SC_LANES)]
                    out = []
                    for r in range(SC_LANES):
                        lo, hi = _unpack_words(rows[grp * SC_LANES + r, pl.ds(c * SC_LANES, SC_LANES)])
                        out.append(accs[r] + lo * h_lo + hi * h_hi)
                    return tuple(out)
                accs = lax.fori_loop(0, PEER_WORDS // SC_LANES, body,
                                     tuple(jnp.zeros((SC_LANES,), F32) for _ in range(SC_LANES)))
                for r in range(SC_LANES):
                    accm[pl.ds(r * SC_LANES, SC_LANES)] = accs[r]
                tot = jnp.zeros((SC_LANES,), F32)
                for l in range(SC_LANES):
                    tot = tot + plsc.load_gather(accm, [lanes * SC_LANES + l])
                a_v[pl.ds(g * PEER_CH + grp * SC_LANES, SC_LANES)] = tot

        @pl.loop(0, tpw // tb)
        def _(blk):
            t0 = tok0 + blk * tb
            pltpu.sync_copy(idx_hbm.at[pl.ds(t0 * nch, tb * nch)], idx_v)
            pltpu.sync_copy(h_hbm.at[pl.ds(t0, tb)], h_v)
            _sc_ring(tb * nch, bufs, sems, gather, compute)
            pltpu.sync_copy(a_v, out_hbm.at[pl.ds(t0 * PEER_E, tb * PEER_E)])

    return k(h3, u_words, idx2, after).reshape(T, PEER_E)


def _peer_sum_sc(x2, w, v_words, eidx):
    T = x2.shape[0]
    tb = _sc_tb(T)
    tpw = T // SC_WORKERS
    nch = PEER_E // PEER_CH
    idx2 = eidx.reshape(T * nch, PEER_CH)
    CG = 16
    ncg = PEER_WORDS // (CG * SC_LANES)

    @functools.partial(
        pl.kernel, mesh=_sc_mesh(),
        out_type=jax.ShapeDtypeStruct((T, D_MODEL), F32),
        scratch_types=[
            pltpu.VMEM((tb * nch, PEER_CH), I32),
            pltpu.VMEM((tb * PEER_E,), F32),
            pltpu.VMEM((tb, D_MODEL), F32),
        ] + [pltpu.VMEM((PEER_CH, PEER_WORDS), I32)] * PEER_NBUF + [pltpu.SemaphoreType.DMA] * PEER_NBUF,
        compiler_params=pltpu.CompilerParams(needs_layout_passes=False),
        name="peer_sum_sc",
    )
    def k(x_hbm, w_hbm, v_hbm, idx_hbm, out_hbm, idx_v, w_v, y_v, *ring):
        bufs, sems = ring[:PEER_NBUF], ring[PEER_NBUF:]
        wid = lax.axis_index("s") * SC_CORES + lax.axis_index("c")
        tok0 = wid * tpw
        zero_i = jnp.zeros((SC_LANES,), I32)

        def gather(g, rows, sem):
            return pltpu.make_async_copy(v_hbm.at[idx_v.at[g]], rows, sem)

        def compute(g, rows):
            tl = g // nch
            for cg in range(ncg):
                base = cg * CG * SC_LANES

                def body(r, accs):
                    wb = plsc.load_gather(w_v, [zero_i + (g * PEER_CH + r)])
                    out_lo, out_hi = [], []
                    for j in range(CG):
                        lo, hi = _unpack_words(rows[r, pl.ds(base + j * SC_LANES, SC_LANES)])
                        out_lo.append(accs[j] + lo * wb)
                        out_hi.append(accs[CG + j] + hi * wb)
                    return tuple(out_lo + out_hi)
                init = tuple([y_v[tl, pl.ds(base + j * SC_LANES, SC_LANES)] for j in range(CG)]
                             + [y_v[tl, pl.ds(PEER_WORDS + base + j * SC_LANES, SC_LANES)] for j in range(CG)])
                accs = lax.fori_loop(0, PEER_CH, body, init)
                for j in range(CG):
                    y_v[tl, pl.ds(base + j * SC_LANES, SC_LANES)] = accs[j]
                    y_v[tl, pl.ds(PEER_WORDS + base + j * SC_LANES, SC_LANES)] = accs[CG + j]

        @pl.loop(0, tpw // tb)
        def _(blk):
            t0 = tok0 + blk * tb
            pltpu.sync_copy(idx_hbm.at[pl.ds(t0 * nch, tb * nch)], idx_v)
            pltpu.sync_copy(w_hbm.at[pl.ds(t0 * PEER_E, tb * PEER_E)], w_v)
            pltpu.sync_copy(x_hbm.at[pl.ds(t0, tb)], y_v)
            _sc_ring(tb * nch, bufs, sems, gather, compute)
            pltpu.sync_copy(y_v, out_hbm.at[pl.ds(t0, tb)])

    return k(x2, w.reshape(-1), v_words, idx2)


def _weight_kernel(g_ref, a_ref, after_ref, w_ref):
    a = a_ref[...]
    w_ref[...] = g_ref[...] * (0.5 * a * (1.0 + lax.erf(a * (1.0 / math.sqrt(2.0)))))


def _peer_weight(g, a, tm, after):
    T = g.shape[0]
    row = lambda i: (i, 0)
    return pl.pallas_call(
        _weight_kernel,
        grid=(T // tm,),
        in_specs=[pl.BlockSpec((tm, PEER_E), row)] * 2 + [pl.BlockSpec(memory_space=pl.ANY)],
        out_specs=pl.BlockSpec((tm, PEER_E), row),
        out_shape=jax.ShapeDtypeStruct((T, PEER_E), F32),
        compiler_params=_cparams(("parallel",)),
        name="peer_weight",
    )(g, a, after)


def _route_group(x2d, ret_o, rg, moba_o, memattn, weights, tm, after):
    (gn_g, gn_b, w_out_bf, norm2_g, w_mq_bf, mem_qn512, ones64, w_mo_bf, norm3_g, wq_bf, sk_hi, sk_lo, _, _) = weights
    x1, qm = _mix(x2d, ret_o, rg, moba_o, gn_g, gn_b, w_out_bf, norm2_g, w_mq_bf, mem_qn512, ones64, tm, after)
    return _route(x1, memattn(qm), w_mo_bf, norm3_g, wq_bf, sk_hi, sk_lo, tm)


def _prompt_groups(n_seq):
    sizes = [n_seq // PROMPT_CHUNKS] * PROMPT_CHUNKS
    assert sum(sizes) == n_seq and PROMPT_CHUNKS >= 2 * SAMPLE_MOBA_PARTS
    return [sizes[i:i + 2] for i in range(0, len(sizes), 2)]


def kernel(x_prompt, x_sample, state_ret, cache_moba_k, cache_moba_v, page_table, cache_mem_k, cache_mem_v, mem_prompt, norm1_g, w_in, ret_gn_g, ret_gn_b, moba_qn_g, moba_kn_g, w_out, norm2_g, mem_norm_g, w_mq, w_mkv, mem_qn_g, mem_kn_g, w_mo, norm3_g, peer_wq, peer_sub_keys, peer_u, peer_v):
    n_b, seq, _ = x_prompt.shape
    n_d, t_dec, _ = x_sample.shape
    past_len = page_table.shape[1] * PAGE_SIZE
    tm = 256

    w_in_bf = w_in.astype(BF16)
    w_out_bf = w_out.astype(BF16)
    w_mq_bf = w_mq.astype(BF16)
    w_mkv_bf = w_mkv.astype(BF16)
    w_mo_bf = w_mo.astype(BF16)
    wq_bf = peer_wq.astype(BF16)
    sk = peer_sub_keys.reshape(2 * PEER_HEADS, PEER_NKEYS, PEER_DQ // 2)
    sk_hi = sk.astype(BF16)
    sk_lo = (sk - sk_hi.astype(F32)).astype(BF16)
    ones64 = _block_ones(512, 64)
    tile8 = lambda g: jnp.tile(g, H_MOBA).reshape(1, 512)
    tile4 = lambda g: jnp.tile(g, H_MEM).reshape(1, 512)
    tail_w = (ret_gn_g, ret_gn_b, w_out_bf, norm2_g, w_mq_bf, tile4(mem_qn_g), ones64, w_mo_bf, norm3_g, wq_bf,
              sk_hi, sk_lo, _pack_bf16_pairs(peer_u), _pack_bf16_pairs(peer_v))

    u_words, v_words = tail_w[-2:]

    xs = x_sample.reshape(n_d * t_dec, D_MODEL)
    srq, srk, srv, srg, smq, smk, smv, _, _ = _proj(xs, norm1_g, w_in_bf, tile8(moba_qn_g), tile8(moba_kn_g), ones64,
                                                    tm, n_d * t_dec, xs)
    rows = 16
    pad = lambda a: jnp.pad(a.reshape(n_d, t_dec, 512), ((0, 0), (0, rows - t_dec), (0, 0))).reshape(n_d * rows, 512)
    sret_o, st_s = _retention(pad(srq), pad(srk), pad(srv), _to_block_diag(state_ret), t_dec, rows, n_d, 1)
    sret_o = sret_o.reshape(n_d, rows, 512)[:, :t_dec].reshape(n_d * t_dec, 512)

    n_sp = n_d // SAMPLE_MOBA_PARTS
    parts, ys, smoba_parts = [], [], []
    s0 = 0
    for gi, sizes in enumerate(_prompt_groups(n_b)):
        routed = []
        for n_c in sizes:
            xp = x_prompt[s0:s0 + n_c].reshape(n_c * seq, D_MODEL)
            rq, rk, rv, rg, mq, mk, mv, mkt, mvt = _proj(xp, norm1_g, w_in_bf, tile8(moba_qn_g), tile8(moba_kn_g),
                                                         ones64, tm, seq, xp)
            ret_o, st_p = _retention(rq, rk, rv, jnp.zeros((n_c, 4, LANES, LANES), F32), RET_CHUNK, RET_CHUNK,
                                     n_c, seq // RET_CHUNK)
            moba_o = _moba_prompt(mq, mk, mv, n_c, seq)
            mem_k, mem_v = _memkv(mem_prompt[s0:s0 + n_c].reshape(n_c * N_MEM, D_MODEL), mem_norm_g,
                                  w_mkv_bf, tile4(mem_kn_g), tm)
            x2, h3, eidx, gate = _route_group(
                xp, ret_o, rg, moba_o,
                functools.partial(_memattn_prompt, mem_k=mem_k, mem_v=mem_v, n_seq=n_c, seq=seq, tq=min(512, seq)),
                tail_w, tm, xp)
            c = len(ys) + len(routed)
            a = _peer_dot_sc(h3, u_words, eidx, ys[c - 2] if c >= 2 else h3)
            routed.append((x2, eidx, gate, a))
            parts.append((st_p, mkt, mvt, mem_k, mem_v))
            s0 += n_c
        k = len(smoba_parts)
        has_part = len(sizes) == 2 and k < SAMPLE_MOBA_PARTS
        if has_part:
            tok = slice(k * n_sp * t_dec, (k + 1) * n_sp * t_dec)
            smoba_parts.append(_moba_sample_part(smq[tok], smk[tok], smv[tok], cache_moba_k, cache_moba_v,
                                                 page_table[k * n_sp:(k + 1) * n_sp], past_len, routed[-1][2]))
        for x2, eidx, gate, a in routed:
            w = _peer_weight(gate, a, tm, smoba_parts[-1] if has_part else gate)
            ys.append(_peer_sum_sc(x2, w, v_words, eidx))

    sx2, sh3, seidx, sgate = _route_group(
        xs, sret_o, srg, jnp.concatenate(smoba_parts, axis=0),
        lambda qm: _memattn_sample(qm, cache_mem_k, cache_mem_v, n_d, t_dec), tail_w, tm, w)
    sa = _peer_dot_sc(sh3, u_words, seidx, sh3)
    y_s = _peer_sum_sc(sx2, _peer_weight(sgate, sa, tm, sgate), v_words, seidx)
    y_p = jnp.concatenate(ys, axis=0)
    st_p, mkt, mvt, mem_k, mem_v = (jnp.concatenate(p, axis=0) for p in zip(*parts))
    kv_out = lambda a: a.reshape(n_b, H_MOBA, HD_MOBA, seq).transpose(0, 3, 1, 2)

    return (y_p.reshape(n_b, seq, D_MODEL), y_s.reshape(n_d, t_dec, D_MODEL),
            _from_block_diag(st_p),
            kv_out(mkt), kv_out(mvt),
            mem_k.reshape(n_b, N_MEM, H_MEM, HD_MEM), mem_v.reshape(n_b, N_MEM, H_MEM, HD_MEM),
            _from_block_diag(st_s),
            smk.reshape(n_d, t_dec, H_MOBA, HD_MOBA), smv.reshape(n_d, t_dec, H_MOBA, HD_MOBA))
```

```python
import functools
import math

import jax
import jax.numpy as jnp
from jax import lax
from jax.experimental import pallas as pl
from jax.experimental.pallas import tpu as pltpu
from jax.experimental.pallas import tpu_sc as plsc

F32 = jnp.float32
BF16 = jnp.bfloat16
I32 = jnp.int32

D_MODEL = 1024
H_RET = 8
DK_RET = 64
RET_CHUNK = 128
RET_W = 512
H_MOBA = 8
HD_MOBA = 64
MOBA_BLOCK = 256
MOBA_TOPK = 3
MOBA_W = 512
IN_W = 3584
N_MEM = 256
H_MEM = 4
HD_MEM = 128
MEM_W = 512
PAGE_SIZE = 128
PEER_HEADS = 8
PEER_NKEYS = 128
PEER_DQ = 256
PEER_TOPK = 16
PEER_E = PEER_HEADS * PEER_TOPK
RMS_EPS = 1e-6
GN_EPS = 1e-5

LANES = 128
SC_CORES = 2
SC_SUBCORES = 16
SC_LANES = 16
SC_WORKERS = SC_CORES * SC_SUBCORES
VMEM_LIMIT = 56 * 1024 * 1024

MOBA_SAMPLE_PPS = 8
PROMPT_CHUNKS = 8
SAMPLE_MOBA_PARTS = 4
SAMPLE_PARTS_PER_WINDOW = 2

NEG = -1e30
NT_DIMS = (((1,), (1,)), ((), ()))


def _split_hi_lo(x):
    hi = x.astype(BF16)
    lo = (x - hi.astype(F32)).astype(BF16)
    return hi, lo


def _nt(a, b):
    return lax.dot_general(a, b, NT_DIMS, preferred_element_type=F32)


def _nt3(a, b):
    ah, al = _split_hi_lo(a)
    bh, bl = _split_hi_lo(b)
    return _nt(ah, bh) + _nt(ah, bl) + _nt(al, bh)


def _group_sum(x, ones_bd):
    hi, lo = _split_hi_lo(x)
    return (jnp.dot(hi, ones_bd, preferred_element_type=F32)
            + jnp.dot(lo, ones_bd, preferred_element_type=F32))


def _rms_rows(x, g):
    return x * lax.rsqrt(jnp.mean(x * x, axis=-1, keepdims=True) + RMS_EPS) * g


def _block_ones(width, group):
    i = jnp.arange(width) // group
    return (i[:, None] == i[None, :]).astype(BF16)


def _cparams(sem, vmem=None):
    return pltpu.CompilerParams(dimension_semantics=sem, vmem_limit_bytes=vmem)


def _proj_kernel(x_ref, g1_ref, w_ref, qn_ref, kn_ref, ones_ref, after_ref,
                 rq_ref, rk_ref, rv_ref, rg_ref, mq_ref, mk_ref, mv_ref, mkt_ref, mvt_ref):
    xn = _rms_rows(x_ref[...], g1_ref[...]).astype(BF16)

    def col(i):
        return jnp.dot(xn, w_ref[:, i * 512:(i + 1) * 512], preferred_element_type=F32)

    rq_ref[...] = col(0)
    rk_ref[...] = col(1) * (DK_RET ** -0.5)
    rv_ref[...] = col(2)
    rg_ref[...] = col(3)
    ones = ones_ref[...]

    def headnorm(a, g):
        ms = _group_sum(a * a, ones) * (1.0 / HD_MOBA)
        return a * lax.rsqrt(ms + RMS_EPS) * g

    mq_ref[...] = headnorm(col(4), qn_ref[...]) * (HD_MOBA ** -0.5)
    mk = headnorm(col(5), kn_ref[...])
    mv = col(6)
    mk_ref[...] = mk
    mv_ref[...] = mv
    mkt_ref[0] = mk.T
    mvt_ref[0] = mv.T


def _proj(x2d, norm1_g, w_in_bf, qn512, kn512, ones64, tm, seq, after):
    T = x2d.shape[0]
    out = jax.ShapeDtypeStruct((T, 512), F32)
    out_t = jax.ShapeDtypeStruct((T // seq, 512, seq), F32)
    per_seq = seq // tm
    row = lambda i: (i, 0)
    row_t = lambda i: (i // per_seq, 0, i % per_seq)
    const = lambda i: (0, 0)
    return pl.pallas_call(
        _proj_kernel,
        grid=(T // tm,),
        in_specs=[pl.BlockSpec((tm, D_MODEL), row), pl.BlockSpec((1, D_MODEL), const),
                  pl.BlockSpec((D_MODEL, IN_W), const), pl.BlockSpec((1, 512), const),
                  pl.BlockSpec((1, 512), const), pl.BlockSpec((512, 512), const),
                  pl.BlockSpec(memory_space=pl.ANY)],
        out_specs=[pl.BlockSpec((tm, 512), row)] * 7 + [pl.BlockSpec((1, 512, tm), row_t)] * 2,
        out_shape=[out] * 7 + [out_t] * 2,
        compiler_params=_cparams(("parallel",), VMEM_LIMIT),
        name="proj",
    )(x2d, norm1_g.reshape(1, -1), w_in_bf, qn512, kn512, ones64, after)


def _ret_kernel(q_ref, k_ref, v_ref, s0_ref, dec_ref, rowdec_ref, kdec_ref, sdec_ref, o_ref, st_ref, state_scr):
    c = pl.program_id(1)
    R = q_ref.shape[0]

    @pl.when(c == 0)
    def _():
        state_scr[...] = s0_ref[0]

    lane = lax.broadcasted_iota(I32, (R, LANES), 1)
    lo_lane = lane < 64
    bd = ((lax.broadcasted_iota(I32, (LANES, LANES), 0) < 64)
          == (lax.broadcasted_iota(I32, (LANES, LANES), 1) < 64))
    for p in range(4):
        sl = slice(p * LANES, (p + 1) * LANES)
        q2 = q_ref[:, sl]
        k2 = k_ref[:, sl]
        v2b = v_ref[:, sl].astype(BF16)
        k2b = k2.astype(BF16)
        qa = jnp.where(lo_lane, q2, 0.0).astype(BF16)
        qb = jnp.where(lo_lane, 0.0, q2).astype(BF16)
        sa = _nt(qa, k2b) * dec_ref[2 * p]
        sb = _nt(qb, k2b) * dec_ref[2 * p + 1]
        ia = jnp.dot(sa.astype(BF16), v2b, preferred_element_type=F32)
        ib = jnp.dot(sb.astype(BF16), v2b, preferred_element_type=F32)
        st = state_scr[p]
        cross = jnp.dot(q2.astype(BF16), st.astype(BF16), preferred_element_type=F32) * rowdec_ref[p]
        o_ref[:, sl] = jnp.where(lo_lane, ia, ib) + cross
        kdt = (k2 * kdec_ref[p]).T.astype(BF16)
        upd = jnp.dot(kdt, v2b, preferred_element_type=F32)
        state_scr[p] = jnp.where(bd, st * sdec_ref[p] + upd, 0.0)

    @pl.when(c == pl.num_programs(1) - 1)
    def _():
        st_ref[0] = state_scr[...]


def _ret_tables(t_real, rows):
    h = jnp.arange(H_RET, dtype=F32)
    log_g = jnp.log1p(-jnp.exp2(-5.0 - h))
    pos = jnp.arange(rows, dtype=F32)
    diff = pos[:, None] - pos[None, :]
    ok = (diff >= 0) & (pos[:, None] < t_real) & (pos[None, :] < t_real)
    dec = jnp.where(ok[None], jnp.exp(log_g[:, None, None] * jnp.where(ok, diff, 0.0)[None]), 0.0)
    lg_lane = jnp.repeat(log_g, 64).reshape(4, 1, LANES)
    rowdec = jnp.exp(lg_lane * (pos[None, :, None] + 1.0))
    kdec = jnp.exp(lg_lane * jnp.maximum(t_real - 1.0 - pos, 0.0)[None, :, None])
    sdec = jnp.exp(lg_lane * float(t_real))
    return dec, rowdec, kdec, sdec


def _to_block_diag(state):
    n = state.shape[0]
    s = state.reshape(n, 4, 2, 64, 64)
    z = jnp.zeros_like(s[:, :, 0])
    top = jnp.concatenate([s[:, :, 0], z], axis=-1)
    bot = jnp.concatenate([z, s[:, :, 1]], axis=-1)
    return jnp.concatenate([top, bot], axis=-2)


def _from_block_diag(sbd):
    n = sbd.shape[0]
    a = sbd[:, :, :64, :64]
    b = sbd[:, :, 64:, 64:]
    return jnp.stack([a, b], axis=2).reshape(n, H_RET, 64, 64)


def _retention(q, k, v, state_bd, t_real, rows, n_seq, n_chunks):
    dec, rowdec, kdec, sdec = _ret_tables(t_real, rows)
    T = q.shape[0]
    tok = lambda s, c: (s * n_chunks + c, 0)
    c3 = lambda s, c: (0, 0, 0)
    st = lambda s, c: (s, 0, 0, 0)
    return pl.pallas_call(
        _ret_kernel,
        grid=(n_seq, n_chunks),
        in_specs=[pl.BlockSpec((rows, 512), tok)] * 3 + [
            pl.BlockSpec((1, 4, LANES, LANES), st),
            pl.BlockSpec((H_RET, rows, rows), c3), pl.BlockSpec((4, rows, LANES), c3),
            pl.BlockSpec((4, rows, LANES), c3), pl.BlockSpec((4, 1, LANES), c3)],
        out_specs=[pl.BlockSpec((rows, 512), tok), pl.BlockSpec((1, 4, LANES, LANES), st)],
        out_shape=[jax.ShapeDtypeStruct((T, 512), F32), jax.ShapeDtypeStruct((n_seq, 4, LANES, LANES), F32)],
        scratch_shapes=[pltpu.VMEM((4, LANES, LANES), F32)],
        compiler_params=_cparams(("parallel", "arbitrary")),
        name="retention",
    )(q, k, v, state_bd, dec, rowdec, kdec, sdec)


def _moba_prompt_kernel(q_ref, k_ref, v_ref, alibi_ref, slope_ref, o_ref,
                        kb_scr, vt_scr, km_scr, bias_scr, m_scr, l_scr, acc_scr):
    i = pl.program_id(2)
    nb = km_scr.shape[0]
    B = MOBA_BLOCK

    @pl.when(i == 0)
    def _():
        for j in range(nb):
            kj = k_ref[j * B:(j + 1) * B, :]
            kb_scr[j * B:(j + 1) * B, :] = kj.astype(BF16)
            km_scr[j:j + 1, :] = jnp.sum(kj, axis=0, keepdims=True) * (1.0 / B)
            vt_scr[:, j * B:(j + 1) * B] = v_ref[j * B:(j + 1) * B, :].T.astype(BF16)

    q2 = q_ref[...]
    lane = lax.broadcasted_iota(I32, (B, LANES), 1)
    qq = jnp.concatenate([jnp.where(lane < 64, q2, 0.0), jnp.where(lane < 64, 0.0, q2)], axis=0)
    qqb = qq.astype(BF16)

    blk = lax.broadcasted_iota(I32, (nb, 2 * B), 0)
    past = blk < i
    g = jnp.where(past, _nt3(km_scr[...], qq), -jnp.inf)
    rank = jnp.zeros((nb, 2 * B), I32)
    for jj in range(nb):
        gj = g[jj:jj + 1, :]
        rank = rank + ((gj > g) | ((gj == g) & (jj < blk))).astype(I32)
    bias_scr[...] = jnp.where(past & (rank < MOBA_TOPK), 0.0, NEG)

    alibi = alibi_ref[0]
    slope_b = slope_ref[0]

    key_r = lax.broadcasted_iota(I32, (B, 2 * B), 0)
    qry_c = lax.broadcasted_iota(I32, (B, 2 * B), 1)
    qry_c = jnp.where(qry_c >= B, qry_c - B, qry_c)
    own = pl.multiple_of(i * B, B)
    s = jnp.where(key_r <= qry_c, _nt(kb_scr[pl.ds(own, B), :], qqb) - alibi, NEG)
    m0 = jnp.max(s, axis=0, keepdims=True)
    p0 = jnp.exp(s - m0)
    m_scr[...] = m0
    l_scr[...] = jnp.sum(p0, axis=0, keepdims=True)
    acc_scr[...] = jnp.dot(vt_scr[:, pl.ds(own, B)], p0.astype(BF16), preferred_element_type=F32)

    def body(j, carry):
        off = pl.multiple_of(j * B, B)
        rowb = bias_scr[pl.ds(j, 1), :] - slope_b * (i - j).astype(F32)
        sj = _nt(kb_scr[pl.ds(off, B), :], qqb) - alibi + rowb
        m_old = m_scr[...]
        m_new = jnp.maximum(m_old, jnp.max(sj, axis=0, keepdims=True))
        alpha = jnp.exp(m_old - m_new)
        pj = jnp.exp(sj - m_new)
        l_scr[...] = alpha * l_scr[...] + jnp.sum(pj, axis=0, keepdims=True)
        acc_scr[...] = alpha * acc_scr[...] + jnp.dot(vt_scr[:, pl.ds(off, B)], pj.astype(BF16),
                                                      preferred_element_type=F32)
        m_scr[...] = m_new
        return carry

    lax.fori_loop(0, i, body, 0)
    res = acc_scr[...] * (1.0 / l_scr[...])
    row = lax.broadcasted_iota(I32, (LANES, B), 0)
    o_ref[...] = jnp.where(row < 64, res[:, :B], res[:, B:]).T


def _moba_tables():
    slopes = jnp.exp2(-8.0 * (jnp.arange(H_MOBA, dtype=F32) + 1.0) / H_MOBA)
    r = jnp.arange(MOBA_BLOCK, dtype=F32)
    rel = r[None, :] - r[:, None]
    alibi = (slopes[:, None, None] * rel[None]).reshape(4, 2, MOBA_BLOCK, MOBA_BLOCK)
    alibi = jnp.concatenate([alibi[:, 0], alibi[:, 1]], axis=-1)
    slope_b = jnp.repeat((slopes * MOBA_BLOCK).reshape(4, 1, 2), MOBA_BLOCK, axis=-1)
    return slopes, alibi, slope_b


def _moba_prompt(mq, mk, mv, n_seq, seq):
    _, alibi, slope_b = _moba_tables()
    nq = seq // MOBA_BLOCK
    T = mq.shape[0]
    B = MOBA_BLOCK
    return pl.pallas_call(
        _moba_prompt_kernel,
        grid=(n_seq, 4, nq),
        in_specs=[pl.BlockSpec((B, LANES), lambda b, p, i: (b * nq + i, p)),
                  pl.BlockSpec((seq, LANES), lambda b, p, i: (b, p)),
                  pl.BlockSpec((seq, LANES), lambda b, p, i: (b, p)),
                  pl.BlockSpec((1, B, 2 * B), lambda b, p, i: (p, 0, 0)),
                  pl.BlockSpec((1, 1, 2 * B), lambda b, p, i: (p, 0, 0))],
        out_specs=pl.BlockSpec((B, LANES), lambda b, p, i: (b * nq + i, p)),
        out_shape=jax.ShapeDtypeStruct((T, 512), F32),
        scratch_shapes=[pltpu.VMEM((seq, LANES), BF16), pltpu.VMEM((LANES, seq), BF16),
                        pltpu.VMEM((nq, LANES), F32), pltpu.VMEM((nq, 2 * B), F32),
                        pltpu.VMEM((1, 2 * B), F32), pltpu.VMEM((1, 2 * B), F32),
                        pltpu.VMEM((LANES, 2 * B), F32)],
        compiler_params=_cparams(("parallel", "parallel", "arbitrary")),
        name="moba_prompt",
    )(mq, mk, mv, alibi, slope_b)


def _moba_sample_kernel(pt_ref, q_ref, kn_ref, vn_ref, *rest, past_len, pps):
    k_refs, v_refs = rest[:pps], rest[pps:2 * pps]
    (alibi_ref, slopec_ref, hm_ref, trow_ref, after_ref, o_ref,
     qbd_scr, knew_scr, vnew_scr, m_scr, l_scr, g_scr, o_scr) = rest[2 * pps:]
    step = pl.program_id(1)
    R = qbd_scr.shape[0]
    lanei = lax.broadcasted_iota(I32, (R, LANES), 1)

    @pl.when(step == 0)
    def _():
        q = q_ref[0]
        hm = hm_ref[...]
        for t in range(4):
            qbd_scr[t * 8:(t + 1) * 8, :] = jnp.broadcast_to(q[t:t + 1, :], (8, 512)) * hm
        knew_scr[...] = jnp.zeros_like(knew_scr)
        vnew_scr[...] = jnp.zeros_like(vnew_scr)
        knew_scr[0:4, :] = kn_ref[0]
        vnew_scr[0:4, :] = vn_ref[0]
        m_scr[...] = jnp.zeros_like(m_scr)
        l_scr[...] = jnp.zeros_like(l_scr)
        g_scr[...] = jnp.zeros_like(g_scr)

    qbd_b = qbd_scr[...].astype(BF16)
    alibi = alibi_ref[...]
    slopec = slopec_ref[...]
    m_all = m_scr[...]
    l_all = l_scr[...]
    g_all = g_scr[...]
    raws = [jnp.dot(qbd_b, k_refs[j][0].astype(BF16), preferred_element_type=F32) for j in range(pps)]
    es = []
    for j in range(pps):
        p = step * pps + j
        off = (past_len - p * PAGE_SIZE).astype(F32)
        s = raws[j] - alibi - slopec * off
        m = jnp.max(s, axis=1, keepdims=True)
        e = jnp.exp(s - m)
        here = lanei == p
        g_all = jnp.where(here, jnp.sum(raws[j], axis=1, keepdims=True), g_all)
        m_all = jnp.where(here, m, m_all)
        l_all = jnp.where(here, jnp.sum(e, axis=1, keepdims=True), l_all)
        es.append(e.astype(BF16))
    for j in range(pps):
        o_scr[step * pps + j] = _nt(es[j], v_refs[j][0].astype(BF16))
    m_scr[...] = m_all
    l_scr[...] = l_all
    g_scr[...] = g_all

    @pl.when(step == pl.num_programs(1) - 1)
    def _():
        n_pg = past_len // PAGE_SIZE
        g = (g_all + pltpu.roll(g_all, LANES - 1, 1)) * (1.0 / MOBA_BLOCK)
        g = jnp.where((lanei < n_pg) & ((lanei & 1) == 0), g, -jnp.inf)
        sel = lanei < 0
        for _ in range(MOBA_TOPK):
            mx = jnp.max(g, axis=1, keepdims=True)
            ix = jnp.min(jnp.where(g == mx, lanei, LANES), axis=1, keepdims=True)
            pick = (lanei == ix) | (lanei == ix + 1)
            sel = sel | pick
            g = jnp.where(pick, -jnp.inf, g)
        sn = _nt(qbd_b, knew_scr[...].astype(BF16)) - alibi_ref[...]
        sn = jnp.where(lanei <= trow_ref[...], sn, NEG)
        mn = jnp.max(sn, axis=1, keepdims=True)
        en = jnp.exp(sn - mn)
        ln = jnp.sum(en, axis=1, keepdims=True)
        on = jnp.dot(en.astype(BF16), vnew_scr[...].astype(BF16), preferred_element_type=F32)
        m_all = m_scr[...]
        mt = jnp.maximum(jnp.max(jnp.where(sel, m_all, -jnp.inf), axis=1, keepdims=True), mn)
        w_all = jnp.where(sel, jnp.exp(m_all - mt), 0.0)
        wn = jnp.exp(mn - mt)
        den = jnp.sum(w_all * l_scr[...], axis=1, keepdims=True) + wn * ln
        num = wn * on
        for pg in range(n_pg):
            num = num + w_all[:, pg:pg + 1] * o_scr[pg]
        of = num * (1.0 / den)
        o_ref[0] = jnp.sum(of.reshape(4, 8, 512) * hm_ref[...][None], axis=1)


def _moba_sample_part(mq, mk, mv, cache_k, cache_v, page_table, past_len, after):
    n_seq, n_pages = page_table.shape
    slopes = jnp.exp2(-8.0 * (jnp.arange(H_MOBA, dtype=F32) + 1.0) / H_MOBA)
    row = jnp.arange(32)
    t_row = (row // 8).astype(F32)
    s_row = slopes[row % 8]
    lane = jnp.arange(LANES, dtype=F32)
    alibi = s_row[:, None] * (t_row[:, None] - lane[None, :])
    slopec = jnp.broadcast_to(s_row[:, None], (32, LANES))
    hm = (jnp.arange(512)[None, :] // 64 == jnp.arange(8)[:, None]).astype(F32)
    trow = jnp.broadcast_to((row // 8).astype(I32)[:, None], (32, LANES))
    n_pool = cache_k.shape[0]
    ck = jnp.transpose(cache_k, (0, 2, 3, 1)).reshape(n_pool, 512, PAGE_SIZE)
    cv = jnp.transpose(cache_v, (0, 2, 3, 1)).reshape(n_pool, 512, PAGE_SIZE)
    pps = MOBA_SAMPLE_PPS
    seq3 = lambda b, s, pt: (b, 0, 0)
    c2 = lambda b, s, pt: (0, 0)
    page_specs = [pl.BlockSpec((1, 512, PAGE_SIZE), lambda b, s, pt, j=j: (pt[b * n_pages + s * pps + j], 0, 0))
                  for j in range(pps)]
    grid_spec = pltpu.PrefetchScalarGridSpec(
        num_scalar_prefetch=1,
        grid=(n_seq, n_pages // pps),
        in_specs=[pl.BlockSpec((1, 4, 512), seq3)] * 3 + page_specs + page_specs + [
            pl.BlockSpec((32, LANES), c2), pl.BlockSpec((32, LANES), c2), pl.BlockSpec((8, 512), c2),
            pl.BlockSpec((32, LANES), c2), pl.BlockSpec(memory_space=pl.ANY)],
        out_specs=pl.BlockSpec((1, 4, 512), seq3),
        scratch_shapes=[pltpu.VMEM((32, 512), F32),
                        pltpu.VMEM((LANES, 512), F32), pltpu.VMEM((LANES, 512), F32),
                        pltpu.VMEM((32, LANES), F32), pltpu.VMEM((32, LANES), F32), pltpu.VMEM((32, LANES), F32),
                        pltpu.VMEM((n_pages, 32, 512), F32)],
    )
    out = pl.pallas_call(
        functools.partial(_moba_sample_kernel, past_len=past_len, pps=pps),
        grid_spec=grid_spec,
        out_shape=jax.ShapeDtypeStruct((n_seq, 4, 512), F32),
        compiler_params=_cparams(("parallel", "arbitrary")),
        name="moba_sample",
    )(page_table.reshape(-1), mq.reshape(n_seq, 4, 512), mk.reshape(n_seq, 4, 512), mv.reshape(n_seq, 4, 512),
      *([ck] * pps), *([cv] * pps), alibi, slopec, hm, trow, after)
    return out.reshape(n_seq * 4, 512)


def _mix_kernel(x_ref, ro_ref, rg_ref, mo_ref, gng_ref, gnb_ref, wout_ref, n2_ref, wmq_ref, qn_ref, ones_ref,
                after_ref, x1_ref, qm_ref):
    ones = ones_ref[...]
    ro = ro_ref[...]
    mu = _group_sum(ro, ones) * (1.0 / 64)
    d = ro - mu
    var = _group_sum(d * d, ones) * (1.0 / 64)
    gn = d * lax.rsqrt(var + GN_EPS)
    rg = rg_ref[...]
    ret_y = (gn * gng_ref[...] + gnb_ref[...]) * (rg * jax.nn.sigmoid(rg))
    y = (jnp.dot(ret_y.astype(BF16), wout_ref[0:512, :], preferred_element_type=F32)
         + jnp.dot(mo_ref[...].astype(BF16), wout_ref[512:1024, :], preferred_element_type=F32))
    x1 = x_ref[...] + y
    x1_ref[...] = x1
    h2 = _rms_rows(x1, n2_ref[...]).astype(BF16)
    qm = jnp.dot(h2, wmq_ref[...], preferred_element_type=F32)
    qn = qn_ref[...]
    for h in range(H_MEM):
        sl = slice(h * HD_MEM, (h + 1) * HD_MEM)
        qm_ref[:, sl] = _rms_rows(qm[:, sl], qn[:, sl]) * (HD_MEM ** -0.5)


def _mix(x2d, ret_o, rg, moba_o, gn_g, gn_b, w_out_bf, norm2_g, w_mq_bf, qn512, ones64, tm, after):
    T = x2d.shape[0]
    row = lambda i: (i, 0)
    const = lambda i: (0, 0)
    return pl.pallas_call(
        _mix_kernel,
        grid=(T // tm,),
        in_specs=[pl.BlockSpec((tm, D_MODEL), row), pl.BlockSpec((tm, 512), row), pl.BlockSpec((tm, 512), row),
                  pl.BlockSpec((tm, 512), row), pl.BlockSpec((1, 512), const), pl.BlockSpec((1, 512), const),
                  pl.BlockSpec((D_MODEL, D_MODEL), const), pl.BlockSpec((1, D_MODEL), const),
                  pl.BlockSpec((D_MODEL, MEM_W), const), pl.BlockSpec((1, MEM_W), const),
                  pl.BlockSpec((512, 512), const), pl.BlockSpec(memory_space=pl.ANY)],
        out_specs=[pl.BlockSpec((tm, D_MODEL), row), pl.BlockSpec((tm, MEM_W), row)],
        out_shape=[jax.ShapeDtypeStruct((T, D_MODEL), F32), jax.ShapeDtypeStruct((T, MEM_W), F32)],
        compiler_params=_cparams(("parallel",), VMEM_LIMIT),
        name="mix_out",
    )(x2d, ret_o, rg, moba_o, gn_g.reshape(1, -1), gn_b.reshape(1, -1), w_out_bf, norm2_g.reshape(1, -1),
      w_mq_bf, qn512, ones64, after)


def _memkv_kernel(m_ref, g_ref, w_ref, kn_ref, k_ref, v_ref):
    mn = _rms_rows(m_ref[...], g_ref[...]).astype(BF16)
    k = jnp.dot(mn, w_ref[:, 0:MEM_W], preferred_element_type=F32)
    kn = kn_ref[...]
    for h in range(H_MEM):
        sl = slice(h * HD_MEM, (h + 1) * HD_MEM)
        k_ref[:, sl] = _rms_rows(k[:, sl], kn[:, sl])
    v_ref[...] = jnp.dot(mn, w_ref[:, MEM_W:2 * MEM_W], preferred_element_type=F32)


def _memkv(mem2d, mem_norm_g, w_mkv_bf, kn512, tm):
    T = mem2d.shape[0]
    row = lambda i: (i, 0)
    const = lambda i: (0, 0)
    out = jax.ShapeDtypeStruct((T, MEM_W), F32)
    return pl.pallas_call(
        _memkv_kernel,
        grid=(T // tm,),
        in_specs=[pl.BlockSpec((tm, D_MODEL), row), pl.BlockSpec((1, D_MODEL), const),
                  pl.BlockSpec((D_MODEL, 2 * MEM_W), const), pl.BlockSpec((1, MEM_W), const)],
        out_specs=[pl.BlockSpec((tm, MEM_W), row)] * 2,
        out_shape=[out, out],
        compiler_params=_cparams(("parallel",)),
        name="mem_kv",
    )(mem2d, mem_norm_g.reshape(1, -1), w_mkv_bf, kn512)


def _memattn_body(q, k, v):
    outs = []
    for h in range(H_MEM):
        sl = slice(h * HD_MEM, (h + 1) * HD_MEM)
        s = _nt(q[:, sl].astype(BF16), k[:, sl].astype(BF16))
        m = jnp.max(s, axis=1, keepdims=True)
        e = jnp.exp(s - m)
        pr = e * (1.0 / jnp.sum(e, axis=1, keepdims=True))
        outs.append(jnp.dot(pr.astype(BF16), v[:, sl].astype(BF16), preferred_element_type=F32))
    return outs


def _memattn_prompt_kernel(q_ref, k_ref, v_ref, o_ref):
    outs = _memattn_body(q_ref[...], k_ref[...], v_ref[...])
    for h in range(H_MEM):
        o_ref[:, h * HD_MEM:(h + 1) * HD_MEM] = outs[h]


def _memattn_prompt(qm, mem_k, mem_v, n_seq, seq, tq):
    T = qm.shape[0]
    nq = seq // tq
    return pl.pallas_call(
        _memattn_prompt_kernel,
        grid=(n_seq, nq),
        in_specs=[pl.BlockSpec((tq, MEM_W), lambda b, i: (b * nq + i, 0)),
                  pl.BlockSpec((N_MEM, MEM_W), lambda b, i: (b, 0)),
                  pl.BlockSpec((N_MEM, MEM_W), lambda b, i: (b, 0))],
        out_specs=pl.BlockSpec((tq, MEM_W), lambda b, i: (b * nq + i, 0)),
        out_shape=jax.ShapeDtypeStruct((T, MEM_W), F32),
        compiler_params=_cparams(("parallel", "arbitrary")),
        name="memattn_prompt",
    )(qm, mem_k, mem_v)


def _memattn_sample_kernel(q_ref, k_ref, v_ref, o_ref):
    outs = _memattn_body(q_ref[0], k_ref[0], v_ref[0])
    for h in range(H_MEM):
        o_ref[0, :, h * HD_MEM:(h + 1) * HD_MEM] = outs[h]


def _memattn_sample(qm, cache_k, cache_v, n_seq, t_dec):
    pad_rows = 16
    q3 = jnp.pad(qm.reshape(n_seq, t_dec, MEM_W), ((0, 0), (0, pad_rows - t_dec), (0, 0)))
    ck = cache_k.reshape(n_seq, N_MEM, MEM_W)
    cv = cache_v.reshape(n_seq, N_MEM, MEM_W)
    b3 = lambda b: (b, 0, 0)
    out = pl.pallas_call(
        _memattn_sample_kernel,
        grid=(n_seq,),
        in_specs=[pl.BlockSpec((1, pad_rows, MEM_W), b3), pl.BlockSpec((1, N_MEM, MEM_W), b3),
                  pl.BlockSpec((1, N_MEM, MEM_W), b3)],
        out_specs=pl.BlockSpec((1, pad_rows, MEM_W), b3),
        out_shape=jax.ShapeDtypeStruct((n_seq, pad_rows, MEM_W), F32),
        compiler_params=_cparams(("parallel",)),
        name="memattn_sample",
    )(q3, ck, cv)
    return out[:, :t_dec].reshape(n_seq * t_dec, MEM_W)


def _topk_rows(s, k, payload=None):
    n = s.shape[0]
    rowf = lax.broadcasted_iota(I32, s.shape, 0).astype(F32)
    vals, idxs, pays = [], [], []
    for _ in range(k):
        mx = jnp.max(s, axis=0, keepdims=True)
        ix = jnp.min(jnp.where(s == mx, rowf, float(n)), axis=0, keepdims=True)
        hit = rowf == ix
        vals.append(mx)
        idxs.append(ix)
        if payload is not None:
            pays.append(jnp.sum(jnp.where(hit, payload, 0.0), axis=0, keepdims=True))
        s = jnp.where(hit, -jnp.inf, s)
    cat = lambda xs: jnp.concatenate(xs, axis=0)
    return cat(vals), cat(idxs), (cat(pays) if payload is not None else None)


def _route_kernel(x1_ref, om_ref, wmo_ref, n3_ref, wq_ref, skh_ref, skl_ref,
                  x2_ref, h3_ref, e_ref, g_ref, q_scr, et_scr, gt_scr):
    x2 = x1_ref[...] + jnp.dot(om_ref[...].astype(BF16), wmo_ref[...], preferred_element_type=F32)
    x2_ref[...] = x2
    h3 = _rms_rows(x2, n3_ref[...])
    h3_ref[...] = h3
    h3b = h3.astype(BF16)
    for j in range(2 * PEER_HEADS):
        q_scr[j] = jnp.dot(h3b, wq_ref[:, j * LANES:(j + 1) * LANES], preferred_element_type=F32)

    K = PEER_TOPK

    def head_body(h, carry):
        sv, si = [], []
        for c in range(2):
            q = q_scr[2 * h + c]
            qh, ql = _split_hi_lo(q)
            kh = skh_ref[2 * h + c]
            s = _nt(kh, qh) + _nt(kh, ql) + _nt(skl_ref[2 * h + c], qh)
            v, ix, _ = _topk_rows(s, K)
            sv.append(v)
            si.append(ix)
        cand = [sv[0][0:1] + sv[1]]
        cidx = [si[0][0:1] * PEER_NKEYS + si[1]]
        for a in range(1, 8):
            cand.append(sv[0][a:a + 1] + sv[1][0:8])
            cidx.append(si[0][a:a + 1] * PEER_NKEYS + si[1][0:8])
        cand.append(sv[0][8:16] + sv[1][0:1])
        cidx.append(si[0][8:16] * PEER_NKEYS + si[1][0:1])
        best, _, eidx = _topk_rows(jnp.concatenate(cand, axis=0), K, jnp.concatenate(cidx, axis=0))
        e = jnp.exp(best - best[0:1])
        gate = e * (1.0 / jnp.sum(e, axis=0, keepdims=True))
        r0 = pl.multiple_of(h * K, K)
        et_scr[pl.ds(r0, K), :] = eidx
        gt_scr[pl.ds(r0, K), :] = gate
        return carry

    lax.fori_loop(0, PEER_HEADS, head_body, 0)
    e_ref[...] = et_scr[...].T.astype(I32)
    g_ref[...] = gt_scr[...].T


def _route(x1, omem, w_mo_bf, norm3_g, wq_bf, sk_hi, sk_lo, tm):
    T = x1.shape[0]
    row = lambda i: (i, 0)
    const = lambda i: (0, 0)
    c3 = lambda i: (0, 0, 0)
    return pl.pallas_call(
        _route_kernel,
        grid=(T // tm,),
        in_specs=[pl.BlockSpec((tm, D_MODEL), row), pl.BlockSpec((tm, MEM_W), row),
                  pl.BlockSpec((MEM_W, D_MODEL), const), pl.BlockSpec((1, D_MODEL), const),
                  pl.BlockSpec((D_MODEL, PEER_HEADS * PEER_DQ), const),
                  pl.BlockSpec((2 * PEER_HEADS, PEER_NKEYS, LANES), c3),
                  pl.BlockSpec((2 * PEER_HEADS, PEER_NKEYS, LANES), c3)],
        out_specs=[pl.BlockSpec((tm, D_MODEL), row), pl.BlockSpec((tm, D_MODEL), row),
                   pl.BlockSpec((tm, PEER_E), row), pl.BlockSpec((tm, PEER_E), row)],
        out_shape=[jax.ShapeDtypeStruct((T, D_MODEL), F32), jax.ShapeDtypeStruct((T, D_MODEL), F32),
                   jax.ShapeDtypeStruct((T, PEER_E), I32), jax.ShapeDtypeStruct((T, PEER_E), F32)],
        scratch_shapes=[pltpu.VMEM((2 * PEER_HEADS, tm, LANES), F32), pltpu.VMEM((PEER_E, tm), F32),
                        pltpu.VMEM((PEER_E, tm), F32)],
        compiler_params=_cparams(("parallel",), VMEM_LIMIT),
        name="peer_route",
    )(x1, omem, w_mo_bf, norm3_g.reshape(1, -1), wq_bf, sk_hi, sk_lo)


PEER_CH = 32
PEER_NBUF = 4
PEER_SC_TB = 32
PEER_WORDS = D_MODEL // 2


def _sc_mesh():
    return plsc.VectorSubcoreMesh(core_axis_name="c", subcore_axis_name="s")


def _sc_tb(n_tokens):
    return min(PEER_SC_TB, n_tokens // SC_WORKERS)


def _sc_ring(n_chunks, bufs, sems, gather, compute):
    nbuf = len(bufs)
    for b in range(nbuf - 1):
        gather(b, bufs[b], sems[b]).start()

    @pl.loop(0, n_chunks, step=nbuf)
    def _(g):
        for b in range(nbuf):
            ahead = (b + nbuf - 1) % nbuf

            @pl.when(g + b + nbuf - 1 < n_chunks)
            def _():
                gather(g + b + nbuf - 1, bufs[ahead], sems[ahead]).start()

            gather(g + b, bufs[b], sems[b]).wait()
            compute(g + b, bufs[b])


def _pack_bf16_pairs(table):
    n, d = table.shape
    tb16 = table.astype(BF16)
    pairs = jnp.stack([tb16[:, :d // 2], tb16[:, d // 2:]], axis=-1)
    return lax.bitcast_convert_type(pairs, I32)


def _unpack_words(wd):
    lo = lax.bitcast_convert_type(lax.shift_left(wd, jnp.full(wd.shape, 16, I32)), F32)
    hi = lax.bitcast_convert_type(wd & jnp.full(wd.shape, -65536, I32), F32)
    return lo, hi


def _peer_dot_sc(h3, u_words, eidx, after):
    T = h3.shape[0]
    tb = _sc_tb(T)
    tpw = T // SC_WORKERS
    nch = PEER_E // PEER_CH
    idx2 = eidx.reshape(T * nch, PEER_CH)

    @functools.partial(
        pl.kernel, mesh=_sc_mesh(),
        out_type=jax.ShapeDtypeStruct((T * PEER_E,), F32),
        scratch_types=[
            pltpu.VMEM((tb * nch, PEER_CH), I32),
            pltpu.VMEM((tb, D_MODEL), F32),
            pltpu.VMEM((tb * PEER_E,), F32),
            pltpu.VMEM((SC_LANES * SC_LANES,), F32),
        ] + [pltpu.VMEM((PEER_CH, PEER_WORDS), I32)] * PEER_NBUF + [pltpu.SemaphoreType.DMA] * PEER_NBUF,
        compiler_params=pltpu.CompilerParams(needs_layout_passes=False),
        name="peer_dot_sc",
    )
    def k(h_hbm, u_hbm, idx_hbm, after_hbm, out_hbm, idx_v, h_v, a_v, accm, *ring):
        bufs, sems = ring[:PEER_NBUF], ring[PEER_NBUF:]
        wid = lax.axis_index("s") * SC_CORES + lax.axis_index("c")
        tok0 = wid * tpw
        lanes = lax.iota(I32, SC_LANES)

        def gather(g, rows, sem):
            return pltpu.make_async_copy(u_hbm.at[idx_v.at[g]], rows, sem)

        def compute(g, rows):
            tl = g // nch
            for grp in range(PEER_CH // SC_LANES):
                def body(c, accs):
                    h_lo = h_v[tl, pl.ds(c * SC_LANES, SC_LANES)]
                    h_hi = h_v[tl, pl.ds(PEER_WORDS + c * SC_LANES, SC_LANES)]
                    out = []
                    for r in range(SC_LANES):
                        lo, hi = _unpack_words(rows[grp * SC_LANES + r, pl.ds(c * SC_LANES, SC_LANES)])
                        out.append(accs[r] + lo * h_lo + hi * h_hi)
                    return tuple(out)
                accs = lax.fori_loop(0, PEER_WORDS // SC_LANES, body,
                                     tuple(jnp.zeros((SC_LANES,), F32) for _ in range(SC_LANES)))
                for r in range(SC_LANES):
                    accm[pl.ds(r * SC_LANES, SC_LANES)] = accs[r]
                tot = jnp.zeros((SC_LANES,), F32)
                for l in range(SC_LANES):
                    tot = tot + plsc.load_gather(accm, [lanes * SC_LANES + l])
                a_v[pl.ds(g * PEER_CH + grp * SC_LANES, SC_LANES)] = tot

        @pl.loop(0, tpw // tb)
        def _(blk):
            t0 = tok0 + blk * tb
            pltpu.sync_copy(idx_hbm.at[pl.ds(t0 * nch, tb * nch)], idx_v)
            pltpu.sync_copy(h_hbm.at[pl.ds(t0, tb)], h_v)
            _sc_ring(tb * nch, bufs, sems, gather, compute)
            pltpu.sync_copy(a_v, out_hbm.at[pl.ds(t0 * PEER_E, tb * PEER_E)])

    return k(h3, u_words, idx2, after).reshape(T, PEER_E)


def _peer_sum_sc(x2, w, v_words, eidx):
    T = x2.shape[0]
    tb = _sc_tb(T)
    tpw = T // SC_WORKERS
    nch = PEER_E // PEER_CH
    idx2 = eidx.reshape(T * nch, PEER_CH)
    CG = 16
    ncg = PEER_WORDS // (CG * SC_LANES)

    @functools.partial(
        pl.kernel, mesh=_sc_mesh(),
        out_type=jax.ShapeDtypeStruct((T, D_MODEL), F32),
        scratch_types=[
            pltpu.VMEM((tb * nch, PEER_CH), I32),
            pltpu.VMEM((tb * PEER_E,), F32),
            pltpu.VMEM((tb, D_MODEL), F32),
        ] + [pltpu.VMEM((PEER_CH, PEER_WORDS), I32)] * PEER_NBUF + [pltpu.SemaphoreType.DMA] * PEER_NBUF,
        compiler_params=pltpu.CompilerParams(needs_layout_passes=False),
        name="peer_sum_sc",
    )
    def k(x_hbm, w_hbm, v_hbm, idx_hbm, out_hbm, idx_v, w_v, y_v, *ring):
        bufs, sems = ring[:PEER_NBUF], ring[PEER_NBUF:]
        wid = lax.axis_index("s") * SC_CORES + lax.axis_index("c")
        tok0 = wid * tpw
        zero_i = jnp.zeros((SC_LANES,), I32)

        def gather(g, rows, sem):
            return pltpu.make_async_copy(v_hbm.at[idx_v.at[g]], rows, sem)

        def compute(g, rows):
            tl = g // nch
            for cg in range(ncg):
                base = cg * CG * SC_LANES

                def body(r, accs):
                    wb = plsc.load_gather(w_v, [zero_i + (g * PEER_CH + r)])
                    out_lo, out_hi = [], []
                    for j in range(CG):
                        lo, hi = _unpack_words(rows[r, pl.ds(base + j * SC_LANES, SC_LANES)])
                        out_lo.append(accs[j] + lo * wb)
                        out_hi.append(accs[CG + j] + hi * wb)
                    return tuple(out_lo + out_hi)
                init = tuple([y_v[tl, pl.ds(base + j * SC_LANES, SC_LANES)] for j in range(CG)]
                             + [y_v[tl, pl.ds(PEER_WORDS + base + j * SC_LANES, SC_LANES)] for j in range(CG)])
                accs = lax.fori_loop(0, PEER_CH, body, init)
                for j in range(CG):
                    y_v[tl, pl.ds(base + j * SC_LANES, SC_LANES)] = accs[j]
                    y_v[tl, pl.ds(PEER_WORDS + base + j * SC_LANES, SC_LANES)] = accs[CG + j]

        @pl.loop(0, tpw // tb)
        def _(blk):
            t0 = tok0 + blk * tb
            pltpu.sync_copy(idx_hbm.at[pl.ds(t0 * nch, tb * nch)], idx_v)
            pltpu.sync_copy(w_hbm.at[pl.ds(t0 * PEER_E, tb * PEER_E)], w_v)
            pltpu.sync_copy(x_hbm.at[pl.ds(t0, tb)], y_v)
            _sc_ring(tb * nch, bufs, sems, gather, compute)
            pltpu.sync_copy(y_v, out_hbm.at[pl.ds(t0, tb)])

    return k(x2, w.reshape(-1), v_words, idx2)


def _weight_kernel(g_ref, a_ref, after_ref, w_ref):
    a = a_ref[...]
    w_ref[...] = g_ref[...] * (0.5 * a * (1.0 + lax.erf(a * (1.0 / math.sqrt(2.0)))))


def _peer_weight(g, a, tm, after):
    T = g.shape[0]
    row = lambda i: (i, 0)
    return pl.pallas_call(
        _weight_kernel,
        grid=(T // tm,),
        in_specs=[pl.BlockSpec((tm, PEER_E), row)] * 2 + [pl.BlockSpec(memory_space=pl.ANY)],
        out_specs=pl.BlockSpec((tm, PEER_E), row),
        out_shape=jax.ShapeDtypeStruct((T, PEER_E), F32),
        compiler_params=_cparams(("parallel",)),
        name="peer_weight",
    )(g, a, after)


def _route_group(x2d, ret_o, rg, moba_o, memattn, weights, tm, after):
    (gn_g, gn_b, w_out_bf, norm2_g, w_mq_bf, mem_qn512, ones64, w_mo_bf, norm3_g, wq_bf, sk_hi, sk_lo, _, _) = weights
    x1, qm = _mix(x2d, ret_o, rg, moba_o, gn_g, gn_b, w_out_bf, norm2_g, w_mq_bf, mem_qn512, ones64, tm, after)
    return _route(x1, memattn(qm), w_mo_bf, norm3_g, wq_bf, sk_hi, sk_lo, tm)


def _prompt_groups(n_seq):
    sizes = [n_seq // PROMPT_CHUNKS] * PROMPT_CHUNKS
    assert sum(sizes) == n_seq and PROMPT_CHUNKS >= 2 * SAMPLE_MOBA_PARTS
    return [sizes[i:i + 2] for i in range(0, len(sizes), 2)]


def kernel(x_prompt, x_sample, state_ret, cache_moba_k, cache_moba_v, page_table, cache_mem_k, cache_mem_v, mem_prompt, norm1_g, w_in, ret_gn_g, ret_gn_b, moba_qn_g, moba_kn_g, w_out, norm2_g, mem_norm_g, w_mq, w_mkv, mem_qn_g, mem_kn_g, w_mo, norm3_g, peer_wq, peer_sub_keys, peer_u, peer_v):
    n_b, seq, _ = x_prompt.shape
    n_d, t_dec, _ = x_sample.shape
    past_len = page_table.shape[1] * PAGE_SIZE
    tm = 256

    w_in_bf = w_in.astype(BF16)
    w_out_bf = w_out.astype(BF16)
    w_mq_bf = w_mq.astype(BF16)
    w_mkv_bf = w_mkv.astype(BF16)
    w_mo_bf = w_mo.astype(BF16)
    wq_bf = peer_wq.astype(BF16)
    sk = peer_sub_keys.reshape(2 * PEER_HEADS, PEER_NKEYS, PEER_DQ // 2)
    sk_hi = sk.astype(BF16)
    sk_lo = (sk - sk_hi.astype(F32)).astype(BF16)
    ones64 = _block_ones(512, 64)
    tile8 = lambda g: jnp.tile(g, H_MOBA).reshape(1, 512)
    tile4 = lambda g: jnp.tile(g, H_MEM).reshape(1, 512)
    tail_w = (ret_gn_g, ret_gn_b, w_out_bf, norm2_g, w_mq_bf, tile4(mem_qn_g), ones64, w_mo_bf, norm3_g, wq_bf,
              sk_hi, sk_lo, _pack_bf16_pairs(peer_u), _pack_bf16_pairs(peer_v))

    u_words, v_words = tail_w[-2:]

    xs = x_sample.reshape(n_d * t_dec, D_MODEL)
    srq, srk, srv, srg, smq, smk, smv, _, _ = _proj(xs, norm1_g, w_in_bf, tile8(moba_qn_g), tile8(moba_kn_g), ones64,
                                                    tm, n_d * t_dec, xs)
    rows = 16
    pad = lambda a: jnp.pad(a.reshape(n_d, t_dec, 512), ((0, 0), (0, rows - t_dec), (0, 0))).reshape(n_d * rows, 512)
    sret_o, st_s = _retention(pad(srq), pad(srk), pad(srv), _to_block_diag(state_ret), t_dec, rows, n_d, 1)
    sret_o = sret_o.reshape(n_d, rows, 512)[:, :t_dec].reshape(n_d * t_dec, 512)

    n_sp = n_d // SAMPLE_MOBA_PARTS
    parts, ys, smoba_parts, sample_routed = [], [], [], None
    s0 = 0
    for gi, sizes in enumerate(_prompt_groups(n_b)):
        routed = []
        for n_c in sizes:
            xp = x_prompt[s0:s0 + n_c].reshape(n_c * seq, D_MODEL)
            rq, rk, rv, rg, mq, mk, mv, mkt, mvt = _proj(xp, norm1_g, w_in_bf, tile8(moba_qn_g), tile8(moba_kn_g),
                                                         ones64, tm, seq, xp)
            ret_o, st_p = _retention(rq, rk, rv, jnp.zeros((n_c, 4, LANES, LANES), F32), RET_CHUNK, RET_CHUNK,
                                     n_c, seq // RET_CHUNK)
            moba_o = _moba_prompt(mq, mk, mv, n_c, seq)
            mem_k, mem_v = _memkv(mem_prompt[s0:s0 + n_c].reshape(n_c * N_MEM, D_MODEL), mem_norm_g,
                                  w_mkv_bf, tile4(mem_kn_g), tm)
            x2, h3, eidx, gate = _route_group(
                xp, ret_o, rg, moba_o,
                functools.partial(_memattn_prompt, mem_k=mem_k, mem_v=mem_v, n_seq=n_c, seq=seq, tq=min(512, seq)),
                tail_w, tm, xp)
            c = len(ys) + len(routed)
            a = _peer_dot_sc(h3, u_words, eidx, ys[c - 2] if c >= 2 else h3)
            routed.append((x2, eidx, gate, a))
            parts.append((st_p, mkt, mvt, mem_k, mem_v))
            s0 += n_c
        pin, rest_here = None, False
        if len(smoba_parts) < SAMPLE_MOBA_PARTS:
            for _ in range(SAMPLE_PARTS_PER_WINDOW):
                k = len(smoba_parts)
                tok = slice(k * n_sp * t_dec, (k + 1) * n_sp * t_dec)
                pin = _moba_sample_part(smq[tok], smk[tok], smv[tok], cache_moba_k, cache_moba_v,
                                        page_table[k * n_sp:(k + 1) * n_sp], past_len,
                                        routed[-1][2] if pin is None else pin)
                smoba_parts.append(pin)
        elif sample_routed is None:
            sample_routed = _route_group(
                xs, sret_o, srg, jnp.concatenate(smoba_parts, axis=0),
                lambda qm: _memattn_sample(qm, cache_mem_k, cache_mem_v, n_d, t_dec), tail_w, tm, routed[-1][2])
            sx2, sh3, seidx, sgate = sample_routed
            sa = _peer_dot_sc(sh3, u_words, seidx, sh3)
            pin, rest_here = sgate, True
        for x2, eidx, gate, a in routed:
            w = _peer_weight(gate, a, tm, gate if pin is None else pin)
            ys.append(_peer_sum_sc(x2, w, v_words, eidx))
        if rest_here:
            y_s = _peer_sum_sc(sx2, _peer_weight(sgate, sa, tm, sgate), v_words, seidx)
    y_p = jnp.concatenate(ys, axis=0)
    st_p, mkt, mvt, mem_k, mem_v = (jnp.concatenate(p, axis=0) for p in zip(*parts))
    kv_out = lambda a: a.reshape(n_b, H_MOBA, HD_MOBA, seq).transpose(0, 3, 1, 2)

    return (y_p.reshape(n_b, seq, D_MODEL), y_s.reshape(n_d, t_dec, D_MODEL),
            _from_block_diag(st_p),
            kv_out(mkt), kv_out(mvt),
            mem_k.reshape(n_b, N_MEM, H_MEM, HD_MEM), mem_v.reshape(n_b, N_MEM, H_MEM, HD_MEM),
            _from_block_diag(st_s),
            smk.reshape(n_d, t_dec, H_MOBA, HD_MOBA), smv.reshape(n_d, t_dec, H_MOBA, HD_MOBA))
```
